```python
import jax, jax.numpy as jnp
from jax import lax
import numpy as np

D_MODEL = 1024
BATCH = 4
SEQ = 8192
DEPTH = 1

N_MEM = 256
MLA_HEADS = 8
MLA_Q_RANK = 256
MLA_KV_RANK = 128
MLA_NOPE_DIM = 64
MLA_ROPE_DIM = 32
MLA_V_DIM = 64
MLA_QK_DIM = MLA_NOPE_DIM + MLA_ROPE_DIM
DSA_HEADS = 8
DSA_KV_HEADS = 2
DSA_HEAD_DIM = 64
IDX_HEADS = 8
IDX_DIM = 64
PARTIAL_ROT_DIM = DSA_HEAD_DIM // 4
INDEX_TOPK_MAX = 256
MEM_HEADS = 4
MEM_HEAD_DIM = 128
D_FF = ((8 * D_MODEL // 3 + 255) // 256) * 256
ROPE_THETA = 500000.0
EPS = 1e-6
QBLOCK = 128
MIX_WIDTH = MLA_HEADS * MLA_V_DIM + DSA_HEADS * DSA_HEAD_DIM
IN_SPLIT_SIZES = (MLA_Q_RANK, MLA_KV_RANK, MLA_ROPE_DIM,
                  DSA_HEADS * DSA_HEAD_DIM, DSA_KV_HEADS * DSA_HEAD_DIM, DSA_KV_HEADS * DSA_HEAD_DIM,
                  IDX_HEADS * IDX_DIM, IDX_DIM, IDX_HEADS)
IN_COLS = sum(IN_SPLIT_SIZES)

kernel_name = 'hybrid_mla_dsa_parallel_heads'


def rmsnorm(x, g):
    x32 = x.astype(jnp.float32)
    y = x32 * lax.rsqrt(jnp.mean(x32 * x32, axis=-1, keepdims=True) + EPS)
    return (y * g.astype(jnp.float32)).astype(x.dtype)


def rope_tables(positions, rot_dim):
    inv_freq = ROPE_THETA ** (-jnp.arange(0, rot_dim, 2, dtype=jnp.float32) / rot_dim)
    ang = positions.astype(jnp.float32)[..., None] * inv_freq
    return jnp.cos(ang)[:, :, None, :], jnp.sin(ang)[:, :, None, :]


def rotate(x, cos, sin):
    x1, x2 = jnp.split(x.astype(jnp.float32), 2, axis=-1)
    return jnp.concatenate([x1 * cos - x2 * sin, x2 * cos + x1 * sin], axis=-1).astype(x.dtype)


def partial_rope(x, cos, sin, rot_dim):
    return jnp.concatenate([rotate(x[..., :rot_dim], cos, sin), x[..., rot_dim:]], axis=-1)


def to_blocks(a, nb):
    return a.reshape((a.shape[0], nb, QBLOCK) + a.shape[2:]).swapaxes(0, 1)


def causal_block_attention(q, k, v, scale):
    B, S, H, _ = q.shape
    nb = S // QBLOCK
    k32 = k.astype(jnp.float32)
    key_pos = jnp.arange(S)

    def one_block(args):
        q_blk, start = args
        q_pos = start + jnp.arange(QBLOCK)
        s = jnp.einsum('bqhd,bshd->bhqs', q_blk.astype(jnp.float32), k32) * scale
        s = jnp.where((key_pos[None, :] <= q_pos[:, None])[None, None], s, -jnp.inf)
        p = jax.nn.softmax(s, axis=-1)
        return jnp.einsum('bhqs,bshd->bqhd', p.astype(v.dtype), v)

    out = lax.map(one_block, (to_blocks(q, nb), jnp.arange(nb) * QBLOCK))
    return out.swapaxes(0, 1).reshape(B, S, H, v.shape[-1])


def dsa_sparse_attention(q, k, v, q_idx, k_idx, w_idx, scale, top_k):
    B, S, H, d = q.shape
    G = k.shape[2]
    R = H // G
    nb = S // QBLOCK
    k_idx32 = k_idx.astype(jnp.float32)
    key_pos = jnp.arange(S)
    gather = jax.vmap(lambda table, idx: table[idx])

    def one_block(args):
        q_blk, qi_blk, wi_blk, start = args
        q_pos = start + jnp.arange(QBLOCK)
        rel = jax.nn.relu(jnp.einsum('bqhd,bsd->bqhs', qi_blk.astype(jnp.float32), k_idx32) * IDX_DIM ** -0.5)
        score = jnp.einsum('bqhs,bqh->bqs', rel, wi_blk.astype(jnp.float32))
        score = jnp.where((key_pos[None, :] <= q_pos[:, None])[None], score, -jnp.inf)
        _, sel = lax.top_k(score, top_k)
        valid = sel <= q_pos[None, :, None]
        k_sel = gather(k, sel)
        v_sel = gather(v, sel)
        qg = q_blk.reshape(B, QBLOCK, G, R, d)
        s = jnp.einsum('bqgrd,bqkgd->bqgrk', qg.astype(jnp.float32), k_sel.astype(jnp.float32)) * scale
        s = jnp.where(valid[:, :, None, None, :], s, -jnp.inf)
        p = jax.nn.softmax(s, axis=-1)
        o = jnp.einsum('bqgrk,bqkgd->bqgrd', p.astype(v.dtype), v_sel)
        return o.reshape(B, QBLOCK, H, d)

    out = lax.map(one_block, (to_blocks(q, nb), to_blocks(q_idx, nb), to_blocks(w_idx, nb),
                              jnp.arange(nb) * QBLOCK))
    return out.swapaxes(0, 1).reshape(B, S, H, d)


def hybrid_mixer(h, cos_m, sin_m, cos_p, sin_p, top_k, w_in, mla_q_a_norm, mla_w_uq, mla_kv_a_norm,
                 mla_w_ukv, mla_q_norm, mla_k_norm, dsa_q_norm, dsa_k_norm, idx_k_norm, w_out):
    B, S, _ = h.shape
    offsets = np.cumsum(IN_SPLIT_SIZES)[:-1].tolist()
    c_q, c_kv, k_pe, q_d, k_d, v_d, q_i, k_i, w_i = jnp.split(h @ w_in, offsets, axis=-1)

    q_m = (rmsnorm(c_q, mla_q_a_norm) @ mla_w_uq).reshape(B, S, MLA_HEADS, MLA_QK_DIM)
    kv_m = (rmsnorm(c_kv, mla_kv_a_norm) @ mla_w_ukv).reshape(B, S, MLA_HEADS, MLA_NOPE_DIM + MLA_V_DIM)
    k_nope, v_m = kv_m[..., :MLA_NOPE_DIM], kv_m[..., MLA_NOPE_DIM:]
    k_m = jnp.concatenate([k_nope, jnp.broadcast_to(k_pe[:, :, None, :], (B, S, MLA_HEADS, MLA_ROPE_DIM))], axis=-1)
    q_m = rmsnorm(q_m, mla_q_norm)
    k_m = rmsnorm(k_m, mla_k_norm)
    q_m = jnp.concatenate([q_m[..., :MLA_NOPE_DIM], rotate(q_m[..., MLA_NOPE_DIM:], cos_m, sin_m)], axis=-1)
    k_m = jnp.concatenate([k_m[..., :MLA_NOPE_DIM], rotate(k_m[..., MLA_NOPE_DIM:], cos_m, sin_m)], axis=-1)
    o_mla = causal_block_attention(q_m, k_m, v_m, MLA_QK_DIM ** -0.5)

    q_d = partial_rope(rmsnorm(q_d.reshape(B, S, DSA_HEADS, DSA_HEAD_DIM), dsa_q_norm), cos_p, sin_p, PARTIAL_ROT_DIM)
    k_d = partial_rope(rmsnorm(k_d.reshape(B, S, DSA_KV_HEADS, DSA_HEAD_DIM), dsa_k_norm), cos_p, sin_p, PARTIAL_ROT_DIM)
    v_d = v_d.reshape(B, S, DSA_KV_HEADS, DSA_HEAD_DIM)
    q_i = partial_rope(q_i.reshape(B, S, IDX_HEADS, IDX_DIM), cos_p, sin_p, PARTIAL_ROT_DIM)
    k_i = partial_rope(rmsnorm(k_i, idx_k_norm)[:, :, None, :], cos_p, sin_p, PARTIAL_ROT_DIM)[:, :, 0, :]
    o_dsa = dsa_sparse_attention(q_d, k_d, v_d, q_i, k_i, w_i * IDX_HEADS ** -0.5, DSA_HEAD_DIM ** -0.5, top_k)

    mixed = jnp.concatenate([o_mla.reshape(B, S, -1), o_dsa.reshape(B, S, -1)], axis=-1)
    return mixed @ w_out


def memory_cross_attention(h, mem_n, w_q, w_k, w_v, q_norm, k_norm, w_o):
    B, S, _ = h.shape
    N = mem_n.shape[1]
    q = rmsnorm((h @ w_q).reshape(B, S, MEM_HEADS, MEM_HEAD_DIM), q_norm)
    k = rmsnorm((mem_n @ w_k).reshape(B, N, MEM_HEADS, MEM_HEAD_DIM), k_norm)
    v = (mem_n @ w_v).reshape(B, N, MEM_HEADS, MEM_HEAD_DIM)
    s = jnp.einsum('bshd,bnhd->bhsn', q.astype(jnp.float32), k.astype(jnp.float32)) * MEM_HEAD_DIM ** -0.5
    p = jax.nn.softmax(s, axis=-1)
    o = jnp.einsum('bhsn,bnhd->bshd', p.astype(v.dtype), v)
    return o.reshape(B, S, MEM_HEADS * MEM_HEAD_DIM) @ w_o


def swiglu_ffn(h, w_gate, w_up, w_down):
    return (jax.nn.silu(h @ w_gate) * (h @ w_up)) @ w_down


def setup_inputs(seed: int = 0) -> dict:
    key = jax.random.key(seed)
    ks = jax.random.split(key, 32)
    f32 = jnp.float32

    def w(k, shape, fan_in):
        return jax.random.normal(k, (DEPTH,) + shape, f32) * fan_in ** -0.5

    def g(k, n):
        return 1.0 + 0.1 * jax.random.normal(k, (DEPTH, n), f32)

    mem_w = MEM_HEADS * MEM_HEAD_DIM
    return {
        'x': jax.random.normal(ks[0], (BATCH, SEQ, D_MODEL), f32),
        'mem': jax.random.normal(ks[1], (BATCH, N_MEM, D_MODEL), f32),
        'positions': (jax.random.randint(ks[2], (BATCH, 1), 0, 1024) + jnp.arange(SEQ)[None, :]).astype(jnp.int32),
        'norm_mix': g(ks[3], D_MODEL),
        'w_in': w(ks[4], (D_MODEL, IN_COLS), D_MODEL),
        'mla_q_a_norm': g(ks[5], MLA_Q_RANK),
        'mla_w_uq': w(ks[6], (MLA_Q_RANK, MLA_HEADS * MLA_QK_DIM), MLA_Q_RANK),
        'mla_kv_a_norm': g(ks[7], MLA_KV_RANK),
        'mla_w_ukv': w(ks[8], (MLA_KV_RANK, MLA_HEADS * (MLA_NOPE_DIM + MLA_V_DIM)), MLA_KV_RANK),
        'mla_q_norm': g(ks[9], MLA_QK_DIM),
        'mla_k_norm': g(ks[10], MLA_QK_DIM),
        'dsa_q_norm': g(ks[11], DSA_HEAD_DIM),
        'dsa_k_norm': g(ks[12], DSA_HEAD_DIM),
        'idx_k_norm': g(ks[13], IDX_DIM),
        'w_out': w(ks[14], (MIX_WIDTH, D_MODEL), MIX_WIDTH),
        'norm_mem_x': g(ks[15], D_MODEL),
        'norm_mem_kv': g(ks[16], D_MODEL),
        'mem_w_q': w(ks[17], (D_MODEL, mem_w), D_MODEL),
        'mem_w_k': w(ks[18], (D_MODEL, mem_w), D_MODEL),
        'mem_w_v': w(ks[19], (D_MODEL, mem_w), D_MODEL),
        'mem_q_norm': g(ks[20], MEM_HEAD_DIM),
        'mem_k_norm': g(ks[21], MEM_HEAD_DIM),
        'mem_w_o': w(ks[22], (mem_w, D_MODEL), mem_w),
        'norm_ffn': g(ks[23], D_MODEL),
        'ffn_w_gate': w(ks[24], (D_MODEL, D_FF), D_MODEL),
        'ffn_w_up': w(ks[25], (D_MODEL, D_FF), D_MODEL),
        'ffn_w_down': w(ks[26], (D_FF, D_MODEL), D_FF),
    }


def reference(x, mem, positions, norm_mix, w_in, mla_q_a_norm, mla_w_uq, mla_kv_a_norm, mla_w_ukv,
              mla_q_norm, mla_k_norm, dsa_q_norm, dsa_k_norm, idx_k_norm, w_out, norm_mem_x, norm_mem_kv,
              mem_w_q, mem_w_k, mem_w_v, mem_q_norm, mem_k_norm, mem_w_o, norm_ffn, ffn_w_gate, ffn_w_up,
              ffn_w_down):
    S = x.shape[1]
    top_k = min(INDEX_TOPK_MAX, S // 4)
    cos_m, sin_m = rope_tables(positions, MLA_ROPE_DIM)
    cos_p, sin_p = rope_tables(positions, PARTIAL_ROT_DIM)
    for l in range(DEPTH):
        h = rmsnorm(x, norm_mix[l])
        x = x + hybrid_mixer(h, cos_m, sin_m, cos_p, sin_p, top_k, w_in[l], mla_q_a_norm[l], mla_w_uq[l],
                             mla_kv_a_norm[l], mla_w_ukv[l], mla_q_norm[l], mla_k_norm[l], dsa_q_norm[l],
                             dsa_k_norm[l], idx_k_norm[l], w_out[l])
        x = x + memory_cross_attention(rmsnorm(x, norm_mem_x[l]), rmsnorm(mem, norm_mem_kv[l]), mem_w_q[l],
                                       mem_w_k[l], mem_w_v[l], mem_q_norm[l], mem_k_norm[l], mem_w_o[l])
        x = x + swiglu_ffn(rmsnorm(x, norm_ffn[l]), ffn_w_gate[l], ffn_w_up[l], ffn_w_down[l])
    return x
```

```python
import functools

import jax
import jax.numpy as jnp
from jax import lax
from jax.experimental import pallas as pl
from jax.experimental.pallas import tpu as pltpu

F32 = jnp.float32
BF16 = jnp.bfloat16

D_MODEL = 1024
MLA_HEADS = 8
MLA_Q_RANK = 256
MLA_KV_RANK = 128
MLA_NOPE_DIM = 64
MLA_ROPE_DIM = 32
MLA_V_DIM = 64
MLA_QK_DIM = MLA_NOPE_DIM + MLA_ROPE_DIM
DSA_HEADS = 8
DSA_KV_HEADS = 2
DSA_GROUP = DSA_HEADS // DSA_KV_HEADS
DSA_HEAD_DIM = 64
IDX_HEADS = 8
IDX_DIM = 64
PARTIAL_ROT_DIM = DSA_HEAD_DIM // 4
INDEX_TOPK_MAX = 256
MEM_HEADS = 4
MEM_HEAD_DIM = 128
D_FF = 2816
ROPE_THETA = 500000.0
EPS = 1e-6

LANES = 128
VMEM_LIMIT_BYTES = 56 * 1024 * 1024

INT_MIN = -2 ** 31
NEG_INF_KEY = -2139095041

_OFF = {}
_o = 0
for _name, _size in (("c_q", MLA_Q_RANK), ("c_kv", MLA_KV_RANK), ("k_pe", MLA_ROPE_DIM),
                     ("q_d", DSA_HEADS * DSA_HEAD_DIM), ("k_d", DSA_KV_HEADS * DSA_HEAD_DIM),
                     ("v_d", DSA_KV_HEADS * DSA_HEAD_DIM), ("q_i", IDX_HEADS * IDX_DIM),
                     ("k_i", IDX_DIM), ("w_i", IDX_HEADS)):
    _OFF[_name] = (_o, _o + _size)
    _o += _size

P_CQ = (0, 256)
P_CKV = (256, 384)
P_KPE = (384, 512)
P_QD = (512, 1024)
P_KD = (1024, 1152)
P_VD = (1152, 1280)
P_QI = (1280, 1792)
P_MISC = (1792, 1920)
P_COLS = 1920


def _nt_dot(a, b):
    return lax.dot_general(a, b, (((1,), (1,)), ((), ())), preferred_element_type=F32)


def _rope(slab, c, a, b, shift):
    up = pltpu.roll(slab, LANES - shift, axis=1)
    dn = pltpu.roll(slab, shift, axis=1)
    return slab * c + up * a + dn * b


def _proj_kernel(x_ref, gmix_ref, win_ref, gqa_ref, wuq_ref, gkva_ref, wukvk_ref, wukvv_ref,
                 gq_ref, gk_ref, gdq_ref, gdk_ref, gik_ref,
                 cm_ref, am_ref, bm_ref, cd_ref, ad_ref, bd_ref,
                 qm_ref, km_ref, vm_ref, qd_ref, kd_ref, vd_ref, qi_ref, ki_ref, wi_ref):
    x = x_ref[0]
    h = x * lax.rsqrt(jnp.mean(x * x, axis=-1, keepdims=True) + EPS) * gmix_ref[...]
    y = jnp.dot(h.astype(BF16), win_ref[...], preferred_element_type=F32)

    cm, am, bm = cm_ref[0], am_ref[0], bm_ref[0]
    cd, ad, bd = cd_ref[0], ad_ref[0], bd_ref[0]
    lane = lax.broadcasted_iota(jnp.int32, (1, LANES), 1)
    lo_half = lane < DSA_HEAD_DIM

    c_q = y[:, P_CQ[0]:P_CQ[1]]
    cqn = c_q * lax.rsqrt(jnp.mean(c_q * c_q, axis=-1, keepdims=True) + EPS) * gqa_ref[...]
    qm_all = jnp.dot(cqn.astype(BF16), wuq_ref[...], preferred_element_type=F32)
    c_kv = y[:, P_CKV[0]:P_CKV[1]]
    ckvn = (c_kv * lax.rsqrt(jnp.mean(c_kv * c_kv, axis=-1, keepdims=True) + EPS)
            * gkva_ref[...]).astype(BF16)
    kn_all = jnp.dot(ckvn, wukvk_ref[...], preferred_element_type=F32)
    vm_all = jnp.dot(ckvn, wukvv_ref[...], preferred_element_type=F32)
    kpe = y[:, P_KPE[0]:P_KPE[1]]
    gq, gk = gq_ref[...], gk_ref[...]
    for hd in range(MLA_HEADS):
        sl = slice(hd * LANES, (hd + 1) * LANES)
        q = qm_all[:, sl]
        q = q * lax.rsqrt(jnp.sum(q * q, axis=-1, keepdims=True) * (1.0 / MLA_QK_DIM) + EPS) * gq
        q = _rope(q, cm, am, bm, MLA_ROPE_DIM // 2) * (MLA_QK_DIM ** -0.5)
        qm_ref[0, hd] = q.astype(BF16)
        k = kn_all[:, sl] + kpe
        k = k * lax.rsqrt(jnp.sum(k * k, axis=-1, keepdims=True) * (1.0 / MLA_QK_DIM) + EPS) * gk
        k = _rope(k, cm, am, bm, MLA_ROPE_DIM // 2)
        km_ref[0, hd] = k.astype(BF16)
        vm_ref[0, hd] = vm_all[:, hd * MLA_V_DIM:(hd + 1) * MLA_V_DIM].astype(BF16)

    def pair_norm(slab, g):
        sq = slab * slab
        s_lo = jnp.sum(jnp.where(lo_half, sq, 0.0), axis=-1, keepdims=True)
        s_hi = jnp.sum(jnp.where(lo_half, 0.0, sq), axis=-1, keepdims=True)
        r = jnp.where(lo_half, lax.rsqrt(s_lo * (1.0 / DSA_HEAD_DIM) + EPS),
                      lax.rsqrt(s_hi * (1.0 / DSA_HEAD_DIM) + EPS))
        return slab * r * g

    rot = PARTIAL_ROT_DIM // 2
    gdq = gdq_ref[...]
    for p in range(DSA_HEADS // 2):
        slab = y[:, P_QD[0] + p * LANES:P_QD[0] + (p + 1) * LANES]
        slab = _rope(pair_norm(slab, gdq), cd, ad, bd, rot) * (DSA_HEAD_DIM ** -0.5)
        qd_ref[0, 2 * p] = slab[:, :DSA_HEAD_DIM].astype(BF16)
        qd_ref[0, 2 * p + 1] = slab[:, DSA_HEAD_DIM:].astype(BF16)
        slab = y[:, P_QI[0] + p * LANES:P_QI[0] + (p + 1) * LANES]
        slab = _rope(slab, cd, ad, bd, rot) * (IDX_DIM ** -0.5)
        qi_ref[0, 2 * p] = slab[:, :IDX_DIM].astype(BF16)
        qi_ref[0, 2 * p + 1] = slab[:, IDX_DIM:].astype(BF16)
    slab = _rope(pair_norm(y[:, P_KD[0]:P_KD[1]], gdk_ref[...]), cd, ad, bd, rot)
    kd_ref[0, 0] = slab[:, :DSA_HEAD_DIM].astype(BF16)
    kd_ref[0, 1] = slab[:, DSA_HEAD_DIM:].astype(BF16)
    v_d = y[:, P_VD[0]:P_VD[1]]
    vd_ref[0, 0] = v_d[:, :DSA_HEAD_DIM].astype(BF16)
    vd_ref[0, 1] = v_d[:, DSA_HEAD_DIM:].astype(BF16)
    misc = y[:, P_MISC[0]:P_MISC[1]]
    s_ki = jnp.sum(jnp.where(lo_half, misc * misc, 0.0), axis=-1, keepdims=True)
    k_i = misc * lax.rsqrt(s_ki * (1.0 / IDX_DIM) + EPS) * gik_ref[...]
    k_i = _rope(k_i, cd, ad, bd, rot)
    ki_ref[0] = k_i[:, :IDX_DIM].astype(BF16)
    wi_ref[0] = misc[:, IDX_DIM:IDX_DIM + IDX_HEADS] * (IDX_HEADS ** -0.5)


def _softmax_step(s, v, m, l, acc):
    m_new = jnp.maximum(m, jnp.max(s, axis=-1, keepdims=True))
    alpha = jnp.exp(m - m_new)
    p = jnp.exp(s - m_new)
    l = alpha * l + jnp.sum(p, axis=-1, keepdims=True)
    acc = alpha * acc + jnp.dot(p.astype(BF16), v, preferred_element_type=F32)
    return m_new, l, acc


def _mla_kernel(q_ref, k_ref, v_ref, o_ref, *, tile, heads_per_step):
    i = pl.program_id(2)
    row = lax.broadcasted_iota(jnp.int32, (tile, tile), 0)
    col = lax.broadcasted_iota(jnp.int32, (tile, tile), 1)
    outs = []
    for hh in range(heads_per_step):
        q = q_ref[0, hh]

        def body(kt, carry, hh=hh, q=q):
            k0 = pl.multiple_of(kt * tile, tile)
            k = k_ref[0, hh, pl.ds(k0, tile), :]
            v = v_ref[0, hh, pl.ds(k0, tile), :]
            return _softmax_step(_nt_dot(q, k), v, *carry)

        init = (jnp.full((tile, 1), -1e30, F32), jnp.zeros((tile, 1), F32),
                jnp.zeros((tile, MLA_V_DIM), F32))
        carry = lax.fori_loop(0, i, body, init)
        k0 = pl.multiple_of(i * tile, tile)
        k = k_ref[0, hh, pl.ds(k0, tile), :]
        v = v_ref[0, hh, pl.ds(k0, tile), :]
        s = jnp.where(col <= row, _nt_dot(q, k), -jnp.inf)
        _, l, acc = _softmax_step(s, v, *carry)
        outs.append(acc / l)
    o_ref[0] = jnp.concatenate(outs, axis=-1).astype(o_ref.dtype)


def _dsa_kernel(qi_ref, wi_ref, ki_ref, qd_ref, kd_ref, vd_ref, o_ref, key_scr, *, tq, tk, top_k):
    i = pl.program_id(1)
    n_diag = tq // tk
    n_full = i * n_diag
    n_tiles = n_full + n_diag
    q_pos = i * tq + lax.broadcasted_iota(jnp.int32, (tq, tk), 0)
    col = lax.broadcasted_iota(jnp.int32, (tq, tk), 1)

    qi = qi_ref[0].reshape(IDX_HEADS * tq, IDX_DIM)
    w = wi_ref[0]

    def score_tile(kt, masked):
        k0 = pl.multiple_of(kt * tk, tk)
        s = _nt_dot(qi, ki_ref[0, pl.ds(k0, tk), :])
        acc = jnp.zeros((tq, tk), F32)
        for hd in range(IDX_HEADS):
            acc = acc + jnp.maximum(s[hd * tq:(hd + 1) * tq], 0.0) * w[:, hd:hd + 1]
        if masked:
            acc = jnp.where(k0 + col <= q_pos, acc, -jnp.inf)
        bits = pltpu.bitcast(acc, jnp.int32)
        bits = jnp.where(bits == INT_MIN, 0, bits)
        key_scr[kt] = bits ^ ((bits >> 31) & 0x7FFFFFFF)

    def score_full(kt, c):
        score_tile(kt, False)
        return c

    lax.fori_loop(0, n_full, score_full, 0)
    for d in range(n_diag):
        score_tile(n_full + d, True)

    def count_ge(cand):
        def body(kt, cnt):
            ind = jnp.where(key_scr[kt] >= cand, 1, 0)
            for c in range(tk // LANES):
                cnt = cnt + ind[:, c * LANES:(c + 1) * LANES]
            return cnt
        cnt = lax.fori_loop(0, n_tiles, body, jnp.zeros((tq, LANES), jnp.int32))
        return jnp.sum(cnt, axis=-1, keepdims=True)

    n_nonneg = count_ge(jnp.zeros((tq, 1), jnp.int32))
    nonneg = n_nonneg >= top_k
    lo0 = jnp.where(nonneg, 0, INT_MIN)
    n_ge0 = jnp.where(nonneg, n_nonneg, n_tiles * tk)

    def bit_step(j, carry):
        lo, n_ge = carry
        cand = lo + jnp.left_shift(jnp.int32(1), 30 - j)
        cnt = count_ge(cand)
        ok = cnt >= top_k
        return jnp.where(ok, cand, lo), jnp.where(ok, cnt, n_ge)

    lo, n_ge = lax.fori_loop(0, 31, bit_step, (lo0, n_ge0))
    n_gt = count_ge(lo + 1)
    quota = jnp.where(lo > NEG_INF_KEY, top_k - n_gt, 0).astype(F32)
    has_ties = jnp.max(n_ge) > top_k
    lo_sel = jnp.maximum(lo, NEG_INF_KEY + 1)

    qd = [qd_ref[0, g * DSA_GROUP:(g + 1) * DSA_GROUP].reshape(DSA_GROUP * tq, DSA_HEAD_DIM)
          for g in range(DSA_KV_HEADS)]

    def attend(with_ties):
        def body(kt, carry):
            eq_seen, state = carry
            k0 = pl.multiple_of(kt * tk, tk)
            keys = key_scr[kt]
            if with_ties:
                eq = keys == lo
                tri = (lax.broadcasted_iota(jnp.int32, (tk, tk), 0)
                       <= lax.broadcasted_iota(jnp.int32, (tk, tk), 1))
                rank = eq_seen + jnp.dot(jnp.where(eq, 1.0, 0.0).astype(BF16),
                                         jnp.where(tri, 1.0, 0.0).astype(BF16),
                                         preferred_element_type=F32)
                sel = jnp.where(eq, rank, jnp.where(keys > lo, 0.0, 3e38)) <= quota
                eq_seen = rank[:, tk - 1:tk]
            else:
                sel = keys >= lo_sel
            new_state = []
            for g in range(DSA_KV_HEADS):
                k = kd_ref[0, g, pl.ds(k0, tk), :]
                v = vd_ref[0, g, pl.ds(k0, tk), :]
                s = _nt_dot(qd[g], k)
                for r in range(DSA_GROUP):
                    s_r = jnp.where(sel, s[r * tq:(r + 1) * tq], -jnp.inf)
                    new_state.append(_softmax_step(s_r, v, *state[g * DSA_GROUP + r]))
            return eq_seen, tuple(new_state)

        init = tuple((jnp.full((tq, 1), -1e30, F32), jnp.zeros((tq, 1), F32),
                      jnp.zeros((tq, DSA_HEAD_DIM), F32)) for _ in range(DSA_HEADS))
        _, state = lax.fori_loop(0, n_tiles, body, (jnp.zeros((tq, 1), F32), init))
        o_ref[0] = jnp.concatenate([acc / l for (_, l, acc) in state], axis=-1).astype(o_ref.dtype)

    @pl.when(has_ties)
    def _():
        attend(True)

    @pl.when(jnp.logical_not(has_ties))
    def _():
        attend(False)


def _memkv_kernel(mem_ref, g_ref, wk_ref, wv_ref, gk_ref, mk_ref, mv_ref):
    m = mem_ref[0]
    mn = (m * lax.rsqrt(jnp.mean(m * m, axis=-1, keepdims=True) + EPS) * g_ref[...]).astype(BF16)
    k = jnp.dot(mn, wk_ref[...], preferred_element_type=F32)
    v = jnp.dot(mn, wv_ref[...], preferred_element_type=F32)
    gk = gk_ref[...]
    for hd in range(MEM_HEADS):
        sl = slice(hd * MEM_HEAD_DIM, (hd + 1) * MEM_HEAD_DIM)
        kh = k[:, sl]
        kh = kh * lax.rsqrt(jnp.mean(kh * kh, axis=-1, keepdims=True) + EPS) * gk
        mk_ref[0, hd] = kh.astype(BF16)
        mv_ref[0, hd] = v[:, sl].astype(BF16)


def _post_kernel(x_ref, om_ref, od_ref, wout_ref, gx_ref, wq_ref, gq_ref, mk_ref, mv_ref, wo_ref,
                 o_ref):
    mixed = jnp.concatenate([om_ref[0], od_ref[0]], axis=-1)
    x1 = x_ref[0] + jnp.dot(mixed, wout_ref[...], preferred_element_type=F32)
    h = (x1 * lax.rsqrt(jnp.mean(x1 * x1, axis=-1, keepdims=True) + EPS) * gx_ref[...]).astype(BF16)
    q = jnp.dot(h, wq_ref[...], preferred_element_type=F32)
    gq = gq_ref[...]
    outs = []
    for hd in range(MEM_HEADS):
        qh = q[:, hd * MEM_HEAD_DIM:(hd + 1) * MEM_HEAD_DIM]
        qh = qh * lax.rsqrt(jnp.mean(qh * qh, axis=-1, keepdims=True) + EPS) * gq
        s = _nt_dot((qh * (MEM_HEAD_DIM ** -0.5)).astype(BF16), mk_ref[0, hd])
        p = jnp.exp(s - jnp.max(s, axis=-1, keepdims=True))
        p = p / jnp.sum(p, axis=-1, keepdims=True)
        outs.append(jnp.dot(p.astype(BF16), mv_ref[0, hd], preferred_element_type=F32))
    o = jnp.concatenate(outs, axis=-1).astype(BF16)
    o_ref[0] = x1 + jnp.dot(o, wo_ref[...], preferred_element_type=F32)


def _ffn_kernel(x_ref, g_ref, wg_ref, wu_ref, wd_ref, o_ref, *, chunk):
    x = x_ref[...]
    h = (x * lax.rsqrt(jnp.mean(x * x, axis=-1, keepdims=True) + EPS) * g_ref[...]).astype(BF16)
    acc = x
    for c in range(D_FF // chunk):
        sl = slice(c * chunk, (c + 1) * chunk)
        gate = jnp.dot(h, wg_ref[:, sl], preferred_element_type=F32)
        up = jnp.dot(h, wu_ref[:, sl], preferred_element_type=F32)
        act = gate * (1.0 / (1.0 + jnp.exp(-gate))) * up
        acc = acc + jnp.dot(act.astype(BF16), wd_ref[sl, :], preferred_element_type=F32)
    o_ref[...] = acc


def _full(shape):
    return pl.BlockSpec(shape, lambda *_: (0,) * len(shape))


def _params(n_axes):
    return pltpu.CompilerParams(dimension_semantics=("arbitrary",) * n_axes,
                                vmem_limit_bytes=VMEM_LIMIT_BYTES)


def _rope_tables(positions):
    pos = positions.astype(F32)[..., None]

    def cs(rot_dim):
        inv_freq = ROPE_THETA ** (-jnp.arange(0, rot_dim, 2, dtype=F32) / rot_dim)
        ang = pos * inv_freq
        return jnp.cos(ang), jnp.sin(ang)

    b, s = positions.shape
    one = lambda n: jnp.ones((b, s, n), F32)
    zero = lambda n: jnp.zeros((b, s, n), F32)
    c, sn = cs(MLA_ROPE_DIM)
    cm = jnp.concatenate([one(64), c, c, one(32)], -1)
    am = jnp.concatenate([zero(64), -sn, zero(48)], -1)
    bm = jnp.concatenate([zero(80), sn, zero(32)], -1)
    c, sn = cs(PARTIAL_ROT_DIM)
    cd = jnp.concatenate([c, c, one(48)] * 2, -1)
    ad = jnp.concatenate([-sn, zero(56)] * 2, -1)
    bd = jnp.concatenate([zero(8), sn, zero(48)] * 2, -1)
    return cm, am, bm, cd, ad, bd


def _pack_weights(w_in, mla_w_uq, mla_w_ukv):
    d = w_in.shape[0]
    z = lambda n: jnp.zeros((d, n), w_in.dtype)
    col = lambda name: w_in[:, _OFF[name][0]:_OFF[name][1]]
    win = jnp.concatenate([col("c_q"), col("c_kv"), z(64), col("k_pe"), z(32), col("q_d"), col("k_d"),
                           col("v_d"), col("q_i"), col("k_i"), col("w_i"), z(56)], axis=1)
    uq = mla_w_uq.reshape(MLA_Q_RANK, MLA_HEADS, MLA_QK_DIM)
    wuq = jnp.pad(uq, ((0, 0), (0, 0), (0, LANES - MLA_QK_DIM))).reshape(MLA_Q_RANK, MLA_HEADS * LANES)
    ukv = mla_w_ukv.reshape(MLA_KV_RANK, MLA_HEADS, MLA_NOPE_DIM + MLA_V_DIM)
    wukvk = jnp.pad(ukv[:, :, :MLA_NOPE_DIM], ((0, 0), (0, 0), (0, LANES - MLA_NOPE_DIM)))
    wukvk = wukvk.reshape(MLA_KV_RANK, MLA_HEADS * LANES)
    wukvv = ukv[:, :, MLA_NOPE_DIM:].reshape(MLA_KV_RANK, MLA_HEADS * MLA_V_DIM)
    return win.astype(BF16), wuq.astype(BF16), wukvk.astype(BF16), wukvv.astype(BF16)


def _pad_gain(g, n):
    return jnp.pad(g, (0, n - g.shape[0])).reshape(1, n)


def kernel(x, mem, positions, norm_mix, w_in, mla_q_a_norm, mla_w_uq, mla_kv_a_norm, mla_w_ukv, mla_q_norm, mla_k_norm, dsa_q_norm, dsa_k_norm, idx_k_norm, w_out, norm_mem_x, norm_mem_kv, mem_w_q, mem_w_k, mem_w_v, mem_q_norm, mem_k_norm, mem_w_o, norm_ffn, ffn_w_gate, ffn_w_up, ffn_w_down):
    B, S, D = x.shape
    n_mem = mem.shape[1]
    top_k = min(INDEX_TOPK_MAX, S // 4)
    depth = norm_mix.shape[0]
    tables = _rope_tables(positions)

    ts = min(512, S)
    t_mla = min(512, S)
    tq_dsa = min(256, S)
    tk_dsa = min(256, S)
    mla_hps = 2
    mem_w = MEM_HEADS * MEM_HEAD_DIM

    for l in range(depth):
        win, wuq, wukvk, wukvv = _pack_weights(w_in[l], mla_w_uq[l], mla_w_ukv[l])
        row = lambda a: a.reshape(1, -1)

        tok = lambda w: pl.BlockSpec((1, ts, w), lambda b, i: (b, i, 0))
        heads = lambda h, w: pl.BlockSpec((1, h, ts, w), lambda b, i: (b, 0, i, 0))
        sds = jax.ShapeDtypeStruct
        proj_out = pl.pallas_call(
            _proj_kernel,
            grid=(B, S // ts),
            in_specs=[tok(D), _full((1, D)), _full((D, P_COLS)), _full((1, MLA_Q_RANK)),
                      _full((MLA_Q_RANK, MLA_HEADS * LANES)), _full((1, MLA_KV_RANK)),
                      _full((MLA_KV_RANK, MLA_HEADS * LANES)), _full((MLA_KV_RANK, MLA_HEADS * MLA_V_DIM)),
                      _full((1, LANES)), _full((1, LANES)), _full((1, LANES)), _full((1, LANES)),
                      _full((1, LANES))] + [tok(LANES)] * 6,
            out_specs=[heads(MLA_HEADS, LANES), heads(MLA_HEADS, LANES), heads(MLA_HEADS, MLA_V_DIM),
                       heads(DSA_HEADS, DSA_HEAD_DIM), heads(DSA_KV_HEADS, DSA_HEAD_DIM),
                       heads(DSA_KV_HEADS, DSA_HEAD_DIM), heads(IDX_HEADS, IDX_DIM),
                       tok(IDX_DIM), tok(IDX_HEADS)],
            out_shape=[sds((B, MLA_HEADS, S, LANES), BF16), sds((B, MLA_HEADS, S, LANES), BF16),
                       sds((B, MLA_HEADS, S, MLA_V_DIM), BF16),
                       sds((B, DSA_HEADS, S, DSA_HEAD_DIM), BF16),
                       sds((B, DSA_KV_HEADS, S, DSA_HEAD_DIM), BF16),
                       sds((B, DSA_KV_HEADS, S, DSA_HEAD_DIM), BF16),
                       sds((B, IDX_HEADS, S, IDX_DIM), BF16),
                       sds((B, S, IDX_DIM), BF16), sds((B, S, IDX_HEADS), F32)],
            compiler_params=_params(2),
            name="proj",
        )(x, row(norm_mix[l]), win, row(mla_q_a_norm[l]), wuq, row(mla_kv_a_norm[l]), wukvk, wukvv,
          _pad_gain(mla_q_norm[l], LANES), _pad_gain(mla_k_norm[l], LANES),
          row(jnp.tile(dsa_q_norm[l], 2)), row(jnp.tile(dsa_k_norm[l], 2)),
          _pad_gain(idx_k_norm[l], LANES), *tables)
        qm, km, vm, qd, kd, vd, qi, ki, wi = proj_out

        o_mla = pl.pallas_call(
            functools.partial(_mla_kernel, tile=t_mla, heads_per_step=mla_hps),
            grid=(B, MLA_HEADS // mla_hps, S // t_mla),
            in_specs=[pl.BlockSpec((1, mla_hps, t_mla, LANES), lambda b, h, i: (b, h, i, 0)),
                      pl.BlockSpec((1, mla_hps, S, LANES), lambda b, h, i: (b, h, 0, 0)),
                      pl.BlockSpec((1, mla_hps, S, MLA_V_DIM), lambda b, h, i: (b, h, 0, 0))],
            out_specs=pl.BlockSpec((1, t_mla, mla_hps * MLA_V_DIM), lambda b, h, i: (b, i, h)),
            out_shape=sds((B, S, MLA_HEADS * MLA_V_DIM), BF16),
            compiler_params=_params(3),
            name="mla_attn",
        )(qm, km, vm)

        o_dsa = pl.pallas_call(
            functools.partial(_dsa_kernel, tq=tq_dsa, tk=tk_dsa, top_k=top_k),
            grid=(B, S // tq_dsa),
            in_specs=[pl.BlockSpec((1, IDX_HEADS, tq_dsa, IDX_DIM), lambda b, i: (b, 0, i, 0)),
                      pl.BlockSpec((1, tq_dsa, IDX_HEADS), lambda b, i: (b, i, 0)),
                      pl.BlockSpec((1, S, IDX_DIM), lambda b, i: (b, 0, 0)),
                      pl.BlockSpec((1, DSA_HEADS, tq_dsa, DSA_HEAD_DIM), lambda b, i: (b, 0, i, 0)),
                      pl.BlockSpec((1, DSA_KV_HEADS, S, DSA_HEAD_DIM), lambda b, i: (b, 0, 0, 0)),
                      pl.BlockSpec((1, DSA_KV_HEADS, S, DSA_HEAD_DIM), lambda b, i: (b, 0, 0, 0))],
            out_specs=pl.BlockSpec((1, tq_dsa, DSA_HEADS * DSA_HEAD_DIM), lambda b, i: (b, i, 0)),
            out_shape=sds((B, S, DSA_HEADS * DSA_HEAD_DIM), BF16),
            scratch_shapes=[pltpu.VMEM((S // tk_dsa, tq_dsa, tk_dsa), jnp.int32)],
            compiler_params=_params(2),
            name="dsa",
        )(qi, wi, ki, qd, kd, vd)

        mk, mv = pl.pallas_call(
            _memkv_kernel,
            grid=(B,),
            in_specs=[pl.BlockSpec((1, n_mem, D), lambda b: (b, 0, 0)), _full((1, D)),
                      _full((D, mem_w)), _full((D, mem_w)), _full((1, MEM_HEAD_DIM))],
            out_specs=[pl.BlockSpec((1, MEM_HEADS, n_mem, MEM_HEAD_DIM), lambda b: (b, 0, 0, 0))] * 2,
            out_shape=[sds((B, MEM_HEADS, n_mem, MEM_HEAD_DIM), BF16)] * 2,
            compiler_params=_params(1),
            name="mem_kv",
        )(mem, row(norm_mem_kv[l]), mem_w_k[l].astype(BF16), mem_w_v[l].astype(BF16),
          row(mem_k_norm[l]))

        memspec = pl.BlockSpec((1, MEM_HEADS, n_mem, MEM_HEAD_DIM), lambda b, i: (b, 0, 0, 0))
        x = pl.pallas_call(
            _post_kernel,
            grid=(B, S // ts),
            in_specs=[tok(D), tok(MLA_HEADS * MLA_V_DIM), tok(DSA_HEADS * DSA_HEAD_DIM),
                      _full((D, D)), _full((1, D)), _full((D, mem_w)), _full((1, MEM_HEAD_DIM)),
                      memspec, memspec, _full((mem_w, D))],
            out_specs=tok(D),
            out_shape=sds((B, S, D), F32),
            compiler_params=_params(2),
            name="post",
        )(x, o_mla, o_dsa, w_out[l].astype(BF16), row(norm_mem_x[l]), mem_w_q[l].astype(BF16),
          row(mem_q_norm[l]), mk, mv, mem_w_o[l].astype(BF16))

        x = pl.pallas_call(
            functools.partial(_ffn_kernel, chunk=D_FF // 2),
            grid=(B * S // ts,),
            in_specs=[pl.BlockSpec((ts, D), lambda i: (i, 0)), _full((1, D)), _full((D, D_FF)),
                      _full((D, D_FF)), _full((D_FF, D))],
            out_specs=pl.BlockSpec((ts, D), lambda i: (i, 0)),
            out_shape=sds((B * S, D), F32),
            compiler_params=_params(1),
            name="ffn",
        )(x.reshape(B * S, D), row(norm_ffn[l]), ffn_w_gate[l].astype(BF16), ffn_w_up[l].astype(BF16),
          ffn_w_down[l].astype(BF16)).reshape(B, S, D)
    return x
```

```python
import functools

import jax
import jax.numpy as jnp
from jax import lax
from jax.experimental import pallas as pl
from jax.experimental.pallas import tpu as pltpu

F32 = jnp.float32
BF16 = jnp.bfloat16

D_MODEL = 1024
MLA_HEADS = 8
MLA_Q_RANK = 256
MLA_KV_RANK = 128
MLA_NOPE_DIM = 64
MLA_ROPE_DIM = 32
MLA_V_DIM = 64
MLA_QK_DIM = MLA_NOPE_DIM + MLA_ROPE_DIM
DSA_HEADS = 8
DSA_KV_HEADS = 2
DSA_GROUP = DSA_HEADS // DSA_KV_HEADS
DSA_HEAD_DIM = 64
IDX_HEADS = 8
IDX_DIM = 64
PARTIAL_ROT_DIM = DSA_HEAD_DIM // 4
INDEX_TOPK_MAX = 256
MEM_HEADS = 4
MEM_HEAD_DIM = 128
D_FF = 2816
ROPE_THETA = 500000.0
EPS = 1e-6
LOG2_E = 1.4426950408889634

LANES = 128
VMEM_LIMIT_BYTES = 56 * 1024 * 1024

INT_MIN = -2 ** 31
NEG_INF_KEY = -2139095041

_OFF = {}
_o = 0
for _name, _size in (("c_q", MLA_Q_RANK), ("c_kv", MLA_KV_RANK), ("k_pe", MLA_ROPE_DIM),
                     ("q_d", DSA_HEADS * DSA_HEAD_DIM), ("k_d", DSA_KV_HEADS * DSA_HEAD_DIM),
                     ("v_d", DSA_KV_HEADS * DSA_HEAD_DIM), ("q_i", IDX_HEADS * IDX_DIM),
                     ("k_i", IDX_DIM), ("w_i", IDX_HEADS)):
    _OFF[_name] = (_o, _o + _size)
    _o += _size

P_CQ = (0, 256)
P_CKV = (256, 384)
P_KPE = (384, 512)
P_QD = (512, 1024)
P_KD = (1024, 1152)
P_VD = (1152, 1280)
P_QI = (1280, 1792)
P_MISC = (1792, 1920)
P_COLS = 1920


def _nt_dot(a, b):
    return lax.dot_general(a, b, (((1,), (1,)), ((), ())), preferred_element_type=F32)


def _rope(slab, c, a, b, shift):
    up = pltpu.roll(slab, LANES - shift, axis=1)
    dn = pltpu.roll(slab, shift, axis=1)
    return slab * c + up * a + dn * b


def _proj_kernel(x_ref, gmix_ref, win_ref, gqa_ref, wuq_ref, gkva_ref, wukvk_ref, wukvv_ref,
                 gq_ref, gk_ref, gdq_ref, gdk_ref, gik_ref,
                 cm_ref, am_ref, bm_ref, cd_ref, ad_ref, bd_ref,
                 qm_ref, km_ref, vm_ref, qd_ref, kd_ref, vd_ref, qi_ref, ki_ref, wi_ref):
    x = x_ref[0]
    h = x * lax.rsqrt(jnp.mean(x * x, axis=-1, keepdims=True) + EPS) * gmix_ref[...]
    y = jnp.dot(h.astype(BF16), win_ref[...], preferred_element_type=F32)

    cm, am, bm = cm_ref[0], am_ref[0], bm_ref[0]
    cd, ad, bd = cd_ref[0], ad_ref[0], bd_ref[0]
    lane = lax.broadcasted_iota(jnp.int32, (1, LANES), 1)
    lo_half = lane < DSA_HEAD_DIM

    c_q = y[:, P_CQ[0]:P_CQ[1]]
    cqn = c_q * lax.rsqrt(jnp.mean(c_q * c_q, axis=-1, keepdims=True) + EPS) * gqa_ref[...]
    qm_all = jnp.dot(cqn.astype(BF16), wuq_ref[...], preferred_element_type=F32)
    c_kv = y[:, P_CKV[0]:P_CKV[1]]
    ckvn = (c_kv * lax.rsqrt(jnp.mean(c_kv * c_kv, axis=-1, keepdims=True) + EPS)
            * gkva_ref[...]).astype(BF16)
    kn_all = jnp.dot(ckvn, wukvk_ref[...], preferred_element_type=F32)
    vm_t = _nt_dot(wukvv_ref[...], ckvn).astype(BF16)
    kpe = y[:, P_KPE[0]:P_KPE[1]]
    gq, gk = gq_ref[...], gk_ref[...]
    for hd in range(MLA_HEADS):
        vm_ref[0, hd, 0] = vm_t[hd * MLA_V_DIM:(hd + 1) * MLA_V_DIM]
    for hd in range(MLA_HEADS):
        sl = slice(hd * LANES, (hd + 1) * LANES)
        q = qm_all[:, sl]
        q = q * lax.rsqrt(jnp.sum(q * q, axis=-1, keepdims=True) * (1.0 / MLA_QK_DIM) + EPS) * gq
        q = _rope(q, cm, am, bm, MLA_ROPE_DIM // 2) * (MLA_QK_DIM ** -0.5 * LOG2_E)
        qm_ref[0, hd] = q.astype(BF16)
        k = kn_all[:, sl] + kpe
        k = k * lax.rsqrt(jnp.sum(k * k, axis=-1, keepdims=True) * (1.0 / MLA_QK_DIM) + EPS) * gk
        k = _rope(k, cm, am, bm, MLA_ROPE_DIM // 2)
        km_ref[0, hd] = k.astype(BF16)

    def pair_norm(slab, g):
        sq = slab * slab
        s_lo = jnp.sum(jnp.where(lo_half, sq, 0.0), axis=-1, keepdims=True)
        s_hi = jnp.sum(jnp.where(lo_half, 0.0, sq), axis=-1, keepdims=True)
        r = jnp.where(lo_half, lax.rsqrt(s_lo * (1.0 / DSA_HEAD_DIM) + EPS),
                      lax.rsqrt(s_hi * (1.0 / DSA_HEAD_DIM) + EPS))
        return slab * r * g

    rot = PARTIAL_ROT_DIM // 2
    gdq = gdq_ref[...]
    for p in range(DSA_HEADS // 2):
        slab = y[:, P_QD[0] + p * LANES:P_QD[0] + (p + 1) * LANES]
        slab = _rope(pair_norm(slab, gdq), cd, ad, bd, rot) * (DSA_HEAD_DIM ** -0.5)
        qd_ref[0, 2 * p] = slab[:, :DSA_HEAD_DIM].astype(BF16)
        qd_ref[0, 2 * p + 1] = slab[:, DSA_HEAD_DIM:].astype(BF16)
        slab = y[:, P_QI[0] + p * LANES:P_QI[0] + (p + 1) * LANES]
        slab = _rope(slab, cd, ad, bd, rot) * (IDX_DIM ** -0.5)
        qi_ref[0, 2 * p] = slab[:, :IDX_DIM].astype(BF16)
        qi_ref[0, 2 * p + 1] = slab[:, IDX_DIM:].astype(BF16)
    slab = _rope(pair_norm(y[:, P_KD[0]:P_KD[1]], gdk_ref[...]), cd, ad, bd, rot)
    kd_ref[0, 0] = slab[:, :DSA_HEAD_DIM].astype(BF16)
    kd_ref[0, 1] = slab[:, DSA_HEAD_DIM:].astype(BF16)
    v_d = y[:, P_VD[0]:P_VD[1]]
    vd_ref[0, 0] = v_d[:, :DSA_HEAD_DIM].astype(BF16)
    vd_ref[0, 1] = v_d[:, DSA_HEAD_DIM:].astype(BF16)
    misc = y[:, P_MISC[0]:P_MISC[1]]
    s_ki = jnp.sum(jnp.where(lo_half, misc * misc, 0.0), axis=-1, keepdims=True)
    k_i = misc * lax.rsqrt(s_ki * (1.0 / IDX_DIM) + EPS) * gik_ref[...]
    k_i = _rope(k_i, cd, ad, bd, rot)
    ki_ref[0] = k_i[:, :IDX_DIM].astype(BF16)
    wi_ref[0] = misc[:, IDX_DIM:IDX_DIM + IDX_HEADS] * (IDX_HEADS ** -0.5)


def _softmax_step(s, v, m, l, acc):
    m_new = jnp.maximum(m, jnp.max(s, axis=-1, keepdims=True))
    alpha = jnp.exp(m - m_new)
    p = jnp.exp(s - m_new)
    l = alpha * l + jnp.sum(p, axis=-1, keepdims=True)
    acc = alpha * acc + jnp.dot(p.astype(BF16), v, preferred_element_type=F32)
    return m_new, l, acc


def _softmax_step_t(s, v_t, m, l, acc):
    m_new = jnp.maximum(m, jnp.max(s, axis=0, keepdims=True))
    alpha = jnp.exp2(m - m_new)
    p = jnp.exp2(s - m_new)
    l = alpha * l + jnp.sum(p, axis=0, keepdims=True)
    acc = alpha * acc + jnp.dot(v_t, p.astype(BF16), preferred_element_type=F32)
    return m_new, l, acc


def _mla_kernel(q_ref, k_ref, vt_ref, o_ref, s_scr, *, tile, heads):
    i = pl.program_id(2)
    qs = [q_ref[0, hd] for hd in range(heads)]
    visible = (lax.broadcasted_iota(jnp.int32, (tile, tile), 0)
               <= lax.broadcasted_iota(jnp.int32, (tile, tile), 1))

    def logits(hd, kt):
        k0 = pl.multiple_of(kt * tile, tile)
        return _nt_dot(k_ref[0, hd, pl.ds(k0, tile), :], qs[hd])

    def key_tile(kt, carry, diag):
        out = []
        s_next = None
        for hd in range(heads):
            s = s_scr[...] if hd == 0 else s_next
            if hd + 1 < heads:
                s_next = logits(hd + 1, kt)
            elif not diag:
                s_scr[...] = logits(0, kt + 1)
            if diag:
                s = jnp.where(visible, s, -jnp.inf)
            out.append(_softmax_step_t(s, vt_ref[0, hd, kt], *carry[hd]))
        return tuple(out)

    s_scr[...] = logits(0, 0)
    init = tuple((jnp.full((1, tile), -1e30, F32), jnp.zeros((1, tile), F32),
                  jnp.zeros((MLA_V_DIM, tile), F32)) for _ in range(heads))
    carry = lax.fori_loop(0, i, lambda kt, c: key_tile(kt, c, False), init)
    carry = key_tile(i, carry, True)
    o_t = jnp.concatenate([acc / l for (_, l, acc) in carry], axis=0)
    o_ref[0] = o_t.T.astype(o_ref.dtype)


def _dsa_kernel(qi_ref, wi_ref, ki_ref, qd_ref, kd_ref, vd_ref, o_ref, key_scr, *, tq, tk, top_k):
    i = pl.program_id(1)
    n_diag = tq // tk
    n_full = i * n_diag
    n_tiles = n_full + n_diag
    q_pos = i * tq + lax.broadcasted_iota(jnp.int32, (tq, tk), 0)
    col = lax.broadcasted_iota(jnp.int32, (tq, tk), 1)

    qi = qi_ref[0].reshape(IDX_HEADS * tq, IDX_DIM)
    w = wi_ref[0]

    def score_tile(kt, masked):
        k0 = pl.multiple_of(kt * tk, tk)
        s = _nt_dot(qi, ki_ref[0, pl.ds(k0, tk), :])
        acc = jnp.zeros((tq, tk), F32)
        for hd in range(IDX_HEADS):
            acc = acc + jnp.maximum(s[hd * tq:(hd + 1) * tq], 0.0) * w[:, hd:hd + 1]
        if masked:
            acc = jnp.where(k0 + col <= q_pos, acc, -jnp.inf)
        bits = pltpu.bitcast(acc, jnp.int32)
        bits = jnp.where(bits == INT_MIN, 0, bits)
        key_scr[kt] = bits ^ ((bits >> 31) & 0x7FFFFFFF)

    def score_full(kt, c):
        score_tile(kt, False)
        return c

    lax.fori_loop(0, n_full, score_full, 0)
    for d in range(n_diag):
        score_tile(n_full + d, True)

    def count_ge(cand):
        def body(kt, cnt):
            ind = jnp.where(key_scr[kt] >= cand, 1, 0)
            for c in range(tk // LANES):
                cnt = cnt + ind[:, c * LANES:(c + 1) * LANES]
            return cnt
        cnt = lax.fori_loop(0, n_tiles, body, jnp.zeros((tq, LANES), jnp.int32))
        return jnp.sum(cnt, axis=-1, keepdims=True)

    n_nonneg = count_ge(jnp.zeros((tq, 1), jnp.int32))
    nonneg = n_nonneg >= top_k
    lo0 = jnp.where(nonneg, 0, INT_MIN)
    n_ge0 = jnp.where(nonneg, n_nonneg, n_tiles * tk)

    def bit_step(j, carry):
        lo, n_ge = carry
        cand = lo + jnp.left_shift(jnp.int32(1), 30 - j)
        cnt = count_ge(cand)
        ok = cnt >= top_k
        return jnp.where(ok, cand, lo), jnp.where(ok, cnt, n_ge)

    lo, n_ge = lax.fori_loop(0, 31, bit_step, (lo0, n_ge0))
    n_gt = count_ge(lo + 1)
    quota = jnp.where(lo > NEG_INF_KEY, top_k - n_gt, 0).astype(F32)
    has_ties = jnp.max(n_ge) > top_k
    lo_sel = jnp.maximum(lo, NEG_INF_KEY + 1)

    qd = [qd_ref[0, g * DSA_GROUP:(g + 1) * DSA_GROUP].reshape(DSA_GROUP * tq, DSA_HEAD_DIM)
          for g in range(DSA_KV_HEADS)]

    def attend(with_ties):
        def body(kt, carry):
            eq_seen, state = carry
            k0 = pl.multiple_of(kt * tk, tk)
            keys = key_scr[kt]
            if with_ties:
                eq = keys == lo
                tri = (lax.broadcasted_iota(jnp.int32, (tk, tk), 0)
                       <= lax.broadcasted_iota(jnp.int32, (tk, tk), 1))
                rank = eq_seen + jnp.dot(jnp.where(eq, 1.0, 0.0).astype(BF16),
                                         jnp.where(tri, 1.0, 0.0).astype(BF16),
                                         preferred_element_type=F32)
                sel = jnp.where(eq, rank, jnp.where(keys > lo, 0.0, 3e38)) <= quota
                eq_seen = rank[:, tk - 1:tk]
            else:
                sel = keys >= lo_sel
            new_state = []
            for g in range(DSA_KV_HEADS):
                k = kd_ref[0, g, pl.ds(k0, tk), :]
                v = vd_ref[0, g, pl.ds(k0, tk), :]
                s = _nt_dot(qd[g], k)
                for r in range(DSA_GROUP):
                    s_r = jnp.where(sel, s[r * tq:(r + 1) * tq], -jnp.inf)
                    new_state.append(_softmax_step(s_r, v, *state[g * DSA_GROUP + r]))
            return eq_seen, tuple(new_state)

        init = tuple((jnp.full((tq, 1), -1e30, F32), jnp.zeros((tq, 1), F32),
                      jnp.zeros((tq, DSA_HEAD_DIM), F32)) for _ in range(DSA_HEADS))
        _, state = lax.fori_loop(0, n_tiles, body, (jnp.zeros((tq, 1), F32), init))
        o_ref[0] = jnp.concatenate([acc / l for (_, l, acc) in state], axis=-1).astype(o_ref.dtype)

    @pl.when(has_ties)
    def _():
        attend(True)

    @pl.when(jnp.logical_not(has_ties))
    def _():
        attend(False)


def _memkv_kernel(mem_ref, g_ref, wk_ref, wv_ref, gk_ref, mk_ref, mv_ref):
    m = mem_ref[0]
    mn = (m * lax.rsqrt(jnp.mean(m * m, axis=-1, keepdims=True) + EPS) * g_ref[...]).astype(BF16)
    k = jnp.dot(mn, wk_ref[...], preferred_element_type=F32)
    v = jnp.dot(mn, wv_ref[...], preferred_element_type=F32)
    gk = gk_ref[...]
    for hd in range(MEM_HEADS):
        sl = slice(hd * MEM_HEAD_DIM, (hd + 1) * MEM_HEAD_DIM)
        kh = k[:, sl]
        kh = kh * lax.rsqrt(jnp.mean(kh * kh, axis=-1, keepdims=True) + EPS) * gk
        mk_ref[0, hd] = kh.astype(BF16)
        mv_ref[0, hd] = v[:, sl].astype(BF16)


def _post_kernel(x_ref, om_ref, od_ref, wout_ref, gx_ref, wq_ref, gq_ref, mk_ref, mv_ref, wo_ref,
                 o_ref):
    mixed = jnp.concatenate([om_ref[0], od_ref[0]], axis=-1)
    x1 = x_ref[0] + jnp.dot(mixed, wout_ref[...], preferred_element_type=F32)
    h = (x1 * lax.rsqrt(jnp.mean(x1 * x1, axis=-1, keepdims=True) + EPS) * gx_ref[...]).astype(BF16)
    q = jnp.dot(h, wq_ref[...], preferred_element_type=F32)
    gq = gq_ref[...]
    outs = []
    for hd in range(MEM_HEADS):
        qh = q[:, hd * MEM_HEAD_DIM:(hd + 1) * MEM_HEAD_DIM]
        qh = qh * lax.rsqrt(jnp.mean(qh * qh, axis=-1, keepdims=True) + EPS) * gq
        s = _nt_dot((qh * (MEM_HEAD_DIM ** -0.5)).astype(BF16), mk_ref[0, hd])
        p = jnp.exp(s - jnp.max(s, axis=-1, keepdims=True))
        p = p / jnp.sum(p, axis=-1, keepdims=True)
        outs.append(jnp.dot(p.astype(BF16), mv_ref[0, hd], preferred_element_type=F32))
    o = jnp.concatenate(outs, axis=-1).astype(BF16)
    o_ref[0] = x1 + jnp.dot(o, wo_ref[...], preferred_element_type=F32)


def _ffn_kernel(x_ref, g_ref, wg_ref, wu_ref, wd_ref, o_ref, *, chunk):
    x = x_ref[...]
    h = (x * lax.rsqrt(jnp.mean(x * x, axis=-1, keepdims=True) + EPS) * g_ref[...]).astype(BF16)
    acc = x
    for c in range(D_FF // chunk):
        sl = slice(c * chunk, (c + 1) * chunk)
        gate = jnp.dot(h, wg_ref[:, sl], preferred_element_type=F32)
        up = jnp.dot(h, wu_ref[:, sl], preferred_element_type=F32)
        act = gate * (1.0 / (1.0 + jnp.exp(-gate))) * up
        acc = acc + jnp.dot(act.astype(BF16), wd_ref[sl, :], preferred_element_type=F32)
    o_ref[...] = acc


def _full(shape):
    return pl.BlockSpec(shape, lambda *_: (0,) * len(shape))


def _params(n_axes):
    return pltpu.CompilerParams(dimension_semantics=("arbitrary",) * n_axes,
                                vmem_limit_bytes=VMEM_LIMIT_BYTES)


def _rope_tables(positions):
    pos = positions.astype(F32)[..., None]

    def cs(rot_dim):
        inv_freq = ROPE_THETA ** (-jnp.arange(0, rot_dim, 2, dtype=F32) / rot_dim)
        ang = pos * inv_freq
        return jnp.cos(ang), jnp.sin(ang)

    b, s = positions.shape
    one = lambda n: jnp.ones((b, s, n), F32)
    zero = lambda n: jnp.zeros((b, s, n), F32)
    c, sn = cs(MLA_ROPE_DIM)
    cm = jnp.concatenate([one(64), c, c, one(32)], -1)
    am = jnp.concatenate([zero(64), -sn, zero(48)], -1)
    bm = jnp.concatenate([zero(80), sn, zero(32)], -1)
    c, sn = cs(PARTIAL_ROT_DIM)
    cd = jnp.concatenate([c, c, one(48)] * 2, -1)
    ad = jnp.concatenate([-sn, zero(56)] * 2, -1)
    bd = jnp.concatenate([zero(8), sn, zero(48)] * 2, -1)
    return cm, am, bm, cd, ad, bd


def _pack_weights(w_in, mla_w_uq, mla_w_ukv):
    d = w_in.shape[0]
    z = lambda n: jnp.zeros((d, n), w_in.dtype)
    col = lambda name: w_in[:, _OFF[name][0]:_OFF[name][1]]
    win = jnp.concatenate([col("c_q"), col("c_kv"), z(64), col("k_pe"), z(32), col("q_d"), col("k_d"),
                           col("v_d"), col("q_i"), col("k_i"), col("w_i"), z(56)], axis=1)
    uq = mla_w_uq.reshape(MLA_Q_RANK, MLA_HEADS, MLA_QK_DIM)
    wuq = jnp.pad(uq, ((0, 0), (0, 0), (0, LANES - MLA_QK_DIM))).reshape(MLA_Q_RANK, MLA_HEADS * LANES)
    ukv = mla_w_ukv.reshape(MLA_KV_RANK, MLA_HEADS, MLA_NOPE_DIM + MLA_V_DIM)
    wukvk = jnp.pad(ukv[:, :, :MLA_NOPE_DIM], ((0, 0), (0, 0), (0, LANES - MLA_NOPE_DIM)))
    wukvk = wukvk.reshape(MLA_KV_RANK, MLA_HEADS * LANES)
    wukvv = ukv[:, :, MLA_NOPE_DIM:].reshape(MLA_KV_RANK, MLA_HEADS * MLA_V_DIM).T
    return win.astype(BF16), wuq.astype(BF16), wukvk.astype(BF16), wukvv.astype(BF16)


def _pad_gain(g, n):
    return jnp.pad(g, (0, n - g.shape[0])).reshape(1, n)


def kernel(x, mem, positions, norm_mix, w_in, mla_q_a_norm, mla_w_uq, mla_kv_a_norm, mla_w_ukv, mla_q_norm, mla_k_norm, dsa_q_norm, dsa_k_norm, idx_k_norm, w_out, norm_mem_x, norm_mem_kv, mem_w_q, mem_w_k, mem_w_v, mem_q_norm, mem_k_norm, mem_w_o, norm_ffn, ffn_w_gate, ffn_w_up, ffn_w_down):
    B, S, D = x.shape
    n_mem = mem.shape[1]
    top_k = min(INDEX_TOPK_MAX, S // 4)
    depth = norm_mix.shape[0]
    tables = _rope_tables(positions)

    ts = min(512, S)
    t_mla = ts
    tq_dsa = min(256, S)
    tk_dsa = min(256, S)
    mla_hps = 4
    mem_w = MEM_HEADS * MEM_HEAD_DIM

    for l in range(depth):
        win, wuq, wukvk, wukvv = _pack_weights(w_in[l], mla_w_uq[l], mla_w_ukv[l])
        row = lambda a: a.reshape(1, -1)

        tok = lambda w: pl.BlockSpec((1, ts, w), lambda b, i: (b, i, 0))
        heads = lambda h, w: pl.BlockSpec((1, h, ts, w), lambda b, i: (b, 0, i, 0))
        sds = jax.ShapeDtypeStruct
        proj_out = pl.pallas_call(
            _proj_kernel,
            grid=(B, S // ts),
            in_specs=[tok(D), _full((1, D)), _full((D, P_COLS)), _full((1, MLA_Q_RANK)),
                      _full((MLA_Q_RANK, MLA_HEADS * LANES)), _full((1, MLA_KV_RANK)),
                      _full((MLA_KV_RANK, MLA_HEADS * LANES)), _full((MLA_HEADS * MLA_V_DIM, MLA_KV_RANK)),
                      _full((1, LANES)), _full((1, LANES)), _full((1, LANES)), _full((1, LANES)),
                      _full((1, LANES))] + [tok(LANES)] * 6,
            out_specs=[heads(MLA_HEADS, LANES), heads(MLA_HEADS, LANES),
                       pl.BlockSpec((1, MLA_HEADS, 1, MLA_V_DIM, ts), lambda b, i: (b, 0, i, 0, 0)),
                       heads(DSA_HEADS, DSA_HEAD_DIM), heads(DSA_KV_HEADS, DSA_HEAD_DIM),
                       heads(DSA_KV_HEADS, DSA_HEAD_DIM), heads(IDX_HEADS, IDX_DIM),
                       tok(IDX_DIM), tok(IDX_HEADS)],
            out_shape=[sds((B, MLA_HEADS, S, LANES), BF16), sds((B, MLA_HEADS, S, LANES), BF16),
                       sds((B, MLA_HEADS, S // ts, MLA_V_DIM, ts), BF16),
                       sds((B, DSA_HEADS, S, DSA_HEAD_DIM), BF16),
                       sds((B, DSA_KV_HEADS, S, DSA_HEAD_DIM), BF16),
                       sds((B, DSA_KV_HEADS, S, DSA_HEAD_DIM), BF16),
                       sds((B, IDX_HEADS, S, IDX_DIM), BF16),
                       sds((B, S, IDX_DIM), BF16), sds((B, S, IDX_HEADS), F32)],
            compiler_params=_params(2),
            name="proj",
        )(x, row(norm_mix[l]), win, row(mla_q_a_norm[l]), wuq, row(mla_kv_a_norm[l]), wukvk, wukvv,
          _pad_gain(mla_q_norm[l], LANES), _pad_gain(mla_k_norm[l], LANES),
          row(jnp.tile(dsa_q_norm[l], 2)), row(jnp.tile(dsa_k_norm[l], 2)),
          _pad_gain(idx_k_norm[l], LANES), *tables)
        qm, km, vm, qd, kd, vd, qi, ki, wi = proj_out

        o_mla = pl.pallas_call(
            functools.partial(_mla_kernel, tile=t_mla, heads=mla_hps),
            grid=(B, MLA_HEADS // mla_hps, S // t_mla),
            in_specs=[pl.BlockSpec((1, mla_hps, t_mla, LANES), lambda b, h, i: (b, h, i, 0)),
                      pl.BlockSpec((1, mla_hps, S, LANES), lambda b, h, i: (b, h, 0, 0)),
                      pl.BlockSpec((1, mla_hps, S // t_mla, MLA_V_DIM, t_mla),
                                   lambda b, h, i: (b, h, 0, 0, 0))],
            out_specs=pl.BlockSpec((1, t_mla, mla_hps * MLA_V_DIM), lambda b, h, i: (b, i, h)),
            out_shape=sds((B, S, MLA_HEADS * MLA_V_DIM), BF16),
            scratch_shapes=[pltpu.VMEM((t_mla, t_mla), F32)],
            compiler_params=_params(3),
            name="mla_attn",
        )(qm, km, vm)

        o_dsa = pl.pallas_call(
            functools.partial(_dsa_kernel, tq=tq_dsa, tk=tk_dsa, top_k=top_k),
            grid=(B, S // tq_dsa),
            in_specs=[pl.BlockSpec((1, IDX_HEADS, tq_dsa, IDX_DIM), lambda b, i: (b, 0, i, 0)),
                      pl.BlockSpec((1, tq_dsa, IDX_HEADS), lambda b, i: (b, i, 0)),
                      pl.BlockSpec((1, S, IDX_DIM), lambda b, i: (b, 0, 0)),
                      pl.BlockSpec((1, DSA_HEADS, tq_dsa, DSA_HEAD_DIM), lambda b, i: (b, 0, i, 0)),
                      pl.BlockSpec((1, DSA_KV_HEADS, S, DSA_HEAD_DIM), lambda b, i: (b, 0, 0, 0)),
                      pl.BlockSpec((1, DSA_KV_HEADS, S, DSA_HEAD_DIM), lambda b, i: (b, 0, 0, 0))],
            out_specs=pl.BlockSpec((1, tq_dsa, DSA_HEADS * DSA_HEAD_DIM), lambda b, i: (b, i, 0)),
            out_shape=sds((B, S, DSA_HEADS * DSA_HEAD_DIM), BF16),
            scratch_shapes=[pltpu.VMEM((S // tk_dsa, tq_dsa, tk_dsa), jnp.int32)],
            compiler_params=_params(2),
            name="dsa",
        )(qi, wi, ki, qd, kd, vd)

        mk, mv = pl.pallas_call(
            _memkv_kernel,
            grid=(B,),
            in_specs=[pl.BlockSpec((1, n_mem, D), lambda b: (b, 0, 0)), _full((1, D)),
                      _full((D, mem_w)), _full((D, mem_w)), _full((1, MEM_HEAD_DIM))],
            out_specs=[pl.BlockSpec((1, MEM_HEADS, n_mem, MEM_HEAD_DIM), lambda b: (b, 0, 0, 0))] * 2,
            out_shape=[sds((B, MEM_HEADS, n_mem, MEM_HEAD_DIM), BF16)] * 2,
            compiler_params=_params(1),
            name="mem_kv",
        )(mem, row(norm_mem_kv[l]), mem_w_k[l].astype(BF16), mem_w_v[l].astype(BF16),
          row(mem_k_norm[l]))

        memspec = pl.BlockSpec((1, MEM_HEADS, n_mem, MEM_HEAD_DIM), lambda b, i: (b, 0, 0, 0))
        x = pl.pallas_call(
            _post_kernel,
            grid=(B, S // ts),
            in_specs=[tok(D), tok(MLA_HEADS * MLA_V_DIM), tok(DSA_HEADS * DSA_HEAD_DIM),
                      _full((D, D)), _full((1, D)), _full((D, mem_w)), _full((1, MEM_HEAD_DIM)),
                      memspec, memspec, _full((mem_w, D))],
            out_specs=tok(D),
            out_shape=sds((B, S, D), F32),
            compiler_params=_params(2),
            name="post",
        )(x, o_mla, o_dsa, w_out[l].astype(BF16), row(norm_mem_x[l]), mem_w_q[l].astype(BF16),
          row(mem_q_norm[l]), mk, mv, mem_w_o[l].astype(BF16))

        x = pl.pallas_call(
            functools.partial(_ffn_kernel, chunk=D_FF // 2),
            grid=(B * S // ts,),
            in_specs=[pl.BlockSpec((ts, D), lambda i: (i, 0)), _full((1, D)), _full((D, D_FF)),
                      _full((D, D_FF)), _full((D_FF, D))],
            out_specs=pl.BlockSpec((ts, D), lambda i: (i, 0)),
            out_shape=sds((B * S, D), F32),
            compiler_params=_params(1),
            name="ffn",
        )(x.reshape(B * S, D), row(norm_ffn[l]), ffn_w_gate[l].astype(BF16), ffn_w_up[l].astype(BF16),
          ffn_w_down[l].astype(BF16)).reshape(B, S, D)
    return x
```

```python
import functools

import jax
import jax.numpy as jnp
from jax import lax
from jax.experimental import pallas as pl
from jax.experimental.pallas import tpu as pltpu

F32 = jnp.float32
BF16 = jnp.bfloat16

D_MODEL = 1024
MLA_HEADS = 8
MLA_Q_RANK = 256
MLA_KV_RANK = 128
MLA_NOPE_DIM = 64
MLA_ROPE_DIM = 32
MLA_V_DIM = 64
MLA_QK_DIM = MLA_NOPE_DIM + MLA_ROPE_DIM
DSA_HEADS = 8
DSA_KV_HEADS = 2
DSA_GROUP = DSA_HEADS // DSA_KV_HEADS
DSA_HEAD_DIM = 64
IDX_HEADS = 8
IDX_DIM = 64
PARTIAL_ROT_DIM = DSA_HEAD_DIM // 4
INDEX_TOPK_MAX = 256
MEM_HEADS = 4
MEM_HEAD_DIM = 128
D_FF = 2816
ROPE_THETA = 500000.0
EPS = 1e-6
LOG2_E = 1.4426950408889634

LANES = 128
VMEM_LIMIT_BYTES = 56 * 1024 * 1024

INT_MIN = -2 ** 31
INT16_MIN, INT16_MAX = -2 ** 15, 2 ** 15 - 1
NEG_INF_KEY = -2139095041
PACKED_ROWS = 16

_OFF = {}
_o = 0
for _name, _size in (("c_q", MLA_Q_RANK), ("c_kv", MLA_KV_RANK), ("k_pe", MLA_ROPE_DIM),
                     ("q_d", DSA_HEADS * DSA_HEAD_DIM), ("k_d", DSA_KV_HEADS * DSA_HEAD_DIM),
                     ("v_d", DSA_KV_HEADS * DSA_HEAD_DIM), ("q_i", IDX_HEADS * IDX_DIM),
                     ("k_i", IDX_DIM), ("w_i", IDX_HEADS)):
    _OFF[_name] = (_o, _o + _size)
    _o += _size

P_CQ = (0, 256)
P_CKV = (256, 384)
P_KPE = (384, 512)
P_QD = (512, 1024)
P_KD = (1024, 1152)
P_QI = (1152, 1664)
P_KI = (1664, 1792)
P_COLS = 1792


def _nt_dot(a, b):
    return lax.dot_general(a, b, (((1,), (1,)), ((), ())), preferred_element_type=F32)


def _rope(slab, c, a, b, shift):
    up = pltpu.roll(slab, LANES - shift, axis=1)
    dn = pltpu.roll(slab, shift, axis=1)
    return slab * c + up * a + dn * b


def _proj_kernel(x_ref, gmix_ref, win_ref, wvd_ref, wwi_ref, gqa_ref, wuq_ref, gkva_ref, wukvk_ref,
                 wukvv_ref, gq_ref, gk_ref, gdq_ref, gdk_ref, gik_ref,
                 cm_ref, am_ref, bm_ref, cd_ref, ad_ref, bd_ref,
                 qm_ref, km_ref, vm_ref, qd_ref, kd_ref, vd_ref, qi_ref, ki_ref, wi_ref, *, dsa_key_block):
    x = x_ref[0]
    h = (x * lax.rsqrt(jnp.mean(x * x, axis=-1, keepdims=True) + EPS) * gmix_ref[...]).astype(BF16)
    y = jnp.dot(h, win_ref[...], preferred_element_type=F32)

    cm, am, bm = cm_ref[0], am_ref[0], bm_ref[0]
    cd, ad, bd = cd_ref[0], ad_ref[0], bd_ref[0]
    lane = lax.broadcasted_iota(jnp.int32, (1, LANES), 1)
    lo_half = lane < DSA_HEAD_DIM

    c_q = y[:, P_CQ[0]:P_CQ[1]]
    cqn = c_q * lax.rsqrt(jnp.mean(c_q * c_q, axis=-1, keepdims=True) + EPS) * gqa_ref[...]
    qm_all = jnp.dot(cqn.astype(BF16), wuq_ref[...], preferred_element_type=F32)
    c_kv = y[:, P_CKV[0]:P_CKV[1]]
    ckvn = (c_kv * lax.rsqrt(jnp.mean(c_kv * c_kv, axis=-1, keepdims=True) + EPS)
            * gkva_ref[...]).astype(BF16)
    kn_all = jnp.dot(ckvn, wukvk_ref[...], preferred_element_type=F32)
    vm_t = _nt_dot(wukvv_ref[...], ckvn).astype(BF16)
    kpe = y[:, P_KPE[0]:P_KPE[1]]
    gq, gk = gq_ref[...], gk_ref[...]
    for hd in range(MLA_HEADS):
        vm_ref[0, hd, 0] = vm_t[hd * MLA_V_DIM:(hd + 1) * MLA_V_DIM]
    for hd in range(MLA_HEADS):
        sl = slice(hd * LANES, (hd + 1) * LANES)
        q = qm_all[:, sl]
        q = q * lax.rsqrt(jnp.sum(q * q, axis=-1, keepdims=True) * (1.0 / MLA_QK_DIM) + EPS) * gq
        q = _rope(q, cm, am, bm, MLA_ROPE_DIM // 2) * (MLA_QK_DIM ** -0.5 * LOG2_E)
        qm_ref[0, hd] = q.astype(BF16)
        k = kn_all[:, sl] + kpe
        k = k * lax.rsqrt(jnp.sum(k * k, axis=-1, keepdims=True) * (1.0 / MLA_QK_DIM) + EPS) * gk
        k = _rope(k, cm, am, bm, MLA_ROPE_DIM // 2)
        km_ref[0, hd] = k.astype(BF16)

    def pair_norm(slab, g):
        sq = slab * slab
        s_lo = jnp.sum(jnp.where(lo_half, sq, 0.0), axis=-1, keepdims=True)
        s_hi = jnp.sum(jnp.where(lo_half, 0.0, sq), axis=-1, keepdims=True)
        r = jnp.where(lo_half, lax.rsqrt(s_lo * (1.0 / DSA_HEAD_DIM) + EPS),
                      lax.rsqrt(s_hi * (1.0 / DSA_HEAD_DIM) + EPS))
        return slab * r * g

    rot = PARTIAL_ROT_DIM // 2
    gdq = gdq_ref[...]
    for p in range(DSA_HEADS // 2):
        slab = y[:, P_QD[0] + p * LANES:P_QD[0] + (p + 1) * LANES]
        slab = _rope(pair_norm(slab, gdq), cd, ad, bd, rot) * (DSA_HEAD_DIM ** -0.5 * LOG2_E)
        qd_ref[0, 2 * p] = slab[:, :DSA_HEAD_DIM].astype(BF16)
        qd_ref[0, 2 * p + 1] = slab[:, DSA_HEAD_DIM:].astype(BF16)
        slab = y[:, P_QI[0] + p * LANES:P_QI[0] + (p + 1) * LANES]
        slab = _rope(slab, cd, ad, bd, rot) * (IDX_DIM ** -0.5)
        qi_ref[0, 2 * p] = slab[:, :IDX_DIM].astype(BF16)
        qi_ref[0, 2 * p + 1] = slab[:, IDX_DIM:].astype(BF16)
    slab = _rope(pair_norm(y[:, P_KD[0]:P_KD[1]], gdk_ref[...]), cd, ad, bd, rot)
    kd_ref[0, 0] = slab[:, :DSA_HEAD_DIM].astype(BF16)
    kd_ref[0, 1] = slab[:, DSA_HEAD_DIM:].astype(BF16)
    vd_t = _nt_dot(wvd_ref[...], h).astype(BF16)
    for g in range(DSA_KV_HEADS):
        for j in range(vd_t.shape[1] // dsa_key_block):
            vd_ref[0, g, j] = vd_t[g * DSA_HEAD_DIM:(g + 1) * DSA_HEAD_DIM,
                                   j * dsa_key_block:(j + 1) * dsa_key_block]
    k_i = y[:, P_KI[0]:P_KI[1]]
    k_i = (k_i * lax.rsqrt(jnp.sum(k_i * k_i, axis=-1, keepdims=True) * (1.0 / IDX_DIM) + EPS)
           * gik_ref[...])
    k_i = _rope(k_i, cd, ad, bd, rot)
    ki_ref[0] = k_i[:, :IDX_DIM].astype(BF16)
    wi_ref[0] = _nt_dot(wwi_ref[...], h)[:IDX_HEADS] * (IDX_HEADS ** -0.5)


def _softmax_step_t(s, v_t, m, l, acc):
    m_new = jnp.maximum(m, jnp.max(s, axis=0, keepdims=True))
    alpha = jnp.exp2(m - m_new)
    p = jnp.exp2(s - m_new)
    l = alpha * l + jnp.sum(p, axis=0, keepdims=True)
    acc = alpha * acc + jnp.dot(v_t, p.astype(BF16), preferred_element_type=F32)
    return m_new, l, acc


def _mla_kernel(q_ref, k_ref, vt_ref, o_ref, s_scr, *, tile, heads):
    i = pl.program_id(2)
    qs = [q_ref[0, hd] for hd in range(heads)]
    visible = (lax.broadcasted_iota(jnp.int32, (tile, tile), 0)
               <= lax.broadcasted_iota(jnp.int32, (tile, tile), 1))

    def logits(hd, kt):
        k0 = pl.multiple_of(kt * tile, tile)
        return _nt_dot(k_ref[0, hd, pl.ds(k0, tile), :], qs[hd])

    def key_tile(kt, carry, diag):
        out = []
        s_next = None
        for hd in range(heads):
            s = s_scr[...] if hd == 0 else s_next
            if hd + 1 < heads:
                s_next = logits(hd + 1, kt)
            elif not diag:
                s_scr[...] = logits(0, kt + 1)
            if diag:
                s = jnp.where(visible, s, -jnp.inf)
            out.append(_softmax_step_t(s, vt_ref[0, hd, kt], *carry[hd]))
        return tuple(out)

    s_scr[...] = logits(0, 0)
    init = tuple((jnp.full((1, tile), -1e30, F32), jnp.zeros((1, tile), F32),
                  jnp.zeros((MLA_V_DIM, tile), F32)) for _ in range(heads))
    carry = lax.fori_loop(0, i, lambda kt, c: key_tile(kt, c, False), init)
    carry = key_tile(i, carry, True)
    o_t = jnp.concatenate([acc / l for (_, l, acc) in carry], axis=0)
    o_ref[0] = o_t.T.astype(o_ref.dtype)


def _dsa_kernel(qi_ref, wi_ref, ki_ref, qd_ref, kd_ref, vdt_ref, o_ref,
                key_scr, hi_scr, lo_scr, s_scr, *, tile, top_k):
    i = pl.program_id(1)
    n_tiles = i + 1
    visible = (lax.broadcasted_iota(jnp.int32, (tile, tile), 0)
               <= lax.broadcasted_iota(jnp.int32, (tile, tile), 1))

    w_t = wi_ref[0]
    pair = 2
    n_pairs = IDX_HEADS // pair

    def idx_logits(p, kt):
        k0 = pl.multiple_of(kt * tile, tile)
        q = qi_ref[0, p * pair:(p + 1) * pair].reshape(pair * tile, IDX_DIM)
        return _nt_dot(ki_ref[0, pl.ds(k0, tile), :], q)

    def score_tile(kt, diag):
        acc = jnp.zeros((tile, tile), F32)
        s_next = idx_logits(0, kt)
        for p in range(n_pairs):
            s = s_next
            if p + 1 < n_pairs:
                s_next = idx_logits(p + 1, kt)
            for j in range(pair):
                hd = p * pair + j
                acc = acc + jnp.maximum(s[:, j * tile:(j + 1) * tile], 0.0) * w_t[hd:hd + 1, :]
        if diag:
            acc = jnp.where(visible, acc, -jnp.inf)
        bits = pltpu.bitcast(acc, jnp.int32)
        bits = jnp.where(bits == INT_MIN, 0, bits)
        key = bits ^ ((bits >> 31) & 0x7FFFFFFF)
        key_scr[kt] = key
        hi_scr[kt] = (key >> 16).astype(jnp.int16)
        lo_scr[kt] = ((key & 0xFFFF) + INT16_MIN).astype(jnp.int16)

    def score_full(kt, c):
        score_tile(kt, False)
        return c

    lax.fori_loop(0, i, score_full, 0)
    score_tile(i, True)

    one16, zero16 = jnp.int16(1), jnp.int16(0)
    n_groups = tile // PACKED_ROWS
    n_acc = 4

    def row16(v):
        return jnp.broadcast_to(v, (PACKED_ROWS, tile)).astype(jnp.int16)

    def count_ge(scr, cand):
        c16 = row16(cand)

        def body(kt, cnts):
            cnts = list(cnts)
            for r in range(n_groups):
                blk = scr[kt, r * PACKED_ROWS:(r + 1) * PACKED_ROWS, :]
                cnts[r % n_acc] = cnts[r % n_acc] + jnp.where(blk >= c16, one16, zero16)
            return tuple(cnts)

        cnts = lax.fori_loop(0, n_tiles, body,
                             tuple(jnp.zeros((PACKED_ROWS, tile), jnp.int16) for _ in range(n_acc)))
        total = cnts[0].astype(jnp.int32)
        for c in cnts[1:]:
            total = total + c.astype(jnp.int32)
        return jnp.sum(total, axis=0, keepdims=True)

    def bisect(scr, need, n_all):
        n0 = count_ge(scr, jnp.zeros((1, tile), jnp.int32))
        ok = n0 >= need
        init = (jnp.where(ok, 0, INT16_MIN), jnp.where(ok, n0, n_all), jnp.where(ok, 0, n0))

        def step(j, carry):
            lo, n_ge, n_gt = carry
            cand = lo + jnp.left_shift(jnp.int32(1), 14 - j)
            cnt = count_ge(scr, cand)
            ok = cnt >= need
            return jnp.where(ok, cand, lo), jnp.where(ok, cnt, n_ge), jnp.where(ok, n_gt, cnt)

        return lax.fori_loop(0, 15, step, init)

    n_stored = jnp.full((1, tile), n_tiles * tile, jnp.int32)
    hi_thr, n_ge_hi, n_gt_hi = bisect(hi_scr, top_k, n_stored)
    hi16 = row16(hi_thr)

    def keep_bucket(kt, c):
        for r in range(n_groups):
            rows = slice(r * PACKED_ROWS, (r + 1) * PACKED_ROWS)
            lo_scr[kt, rows, :] = jnp.where(hi_scr[kt, rows, :] == hi16, lo_scr[kt, rows, :],
                                            jnp.int16(INT16_MIN))
        return c

    lax.fori_loop(0, n_tiles, keep_bucket, 0)
    lo_thr, n_ge_lo, n_gt_lo = bisect(lo_scr, top_k - n_gt_hi, n_ge_hi - n_gt_hi)
    thr = hi_thr * 65536 + (lo_thr - INT16_MIN)
    n_gt = n_gt_hi + n_gt_lo
    quota = jnp.where(thr > NEG_INF_KEY, top_k - n_gt, 0).astype(F32)
    has_ties = jnp.max(n_gt_hi + n_ge_lo) > top_k
    lo_sel = jnp.maximum(thr, NEG_INF_KEY + 1)

    def logits(g, kt):
        k0 = pl.multiple_of(kt * tile, tile)
        q = qd_ref[0, g * DSA_GROUP:(g + 1) * DSA_GROUP].reshape(DSA_GROUP * tile, DSA_HEAD_DIM)
        return _nt_dot(kd_ref[0, g, pl.ds(k0, tile), :], q)

    def attend(with_ties):
        def key_tile(kt, carry, last):
            eq_seen, state = carry
            keys = key_scr[kt]
            if with_ties:
                eq = keys == thr
                tri = (lax.broadcasted_iota(jnp.int32, (tile, tile), 1)
                       <= lax.broadcasted_iota(jnp.int32, (tile, tile), 0))
                rank = eq_seen + jnp.dot(jnp.where(tri, 1.0, 0.0).astype(BF16),
                                         jnp.where(eq, 1.0, 0.0).astype(BF16),
                                         preferred_element_type=F32)
                sel = jnp.where(eq, rank, jnp.where(keys > thr, 0.0, 3e38)) <= quota
                eq_seen = rank[tile - 1:tile, :]
            else:
                sel = keys >= lo_sel
            s_g = [s_scr[...], logits(1, kt)]
            new_state = []
            for g in range(DSA_KV_HEADS):
                if g == 1 and not last:
                    s_scr[...] = logits(0, kt + 1)
                for r in range(DSA_GROUP):
                    s = jnp.where(sel, s_g[g][:, r * tile:(r + 1) * tile], -jnp.inf)
                    new_state.append(_softmax_step_t(s, vdt_ref[0, g, kt], *state[g * DSA_GROUP + r]))
            return eq_seen, tuple(new_state)

        s_scr[...] = logits(0, 0)
        init = tuple((jnp.full((1, tile), -1e30, F32), jnp.zeros((1, tile), F32),
                      jnp.zeros((DSA_HEAD_DIM, tile), F32)) for _ in range(DSA_HEADS))
        carry = lax.fori_loop(0, i, lambda kt, c: key_tile(kt, c, False),
                              (jnp.zeros((1, tile), F32), init))
        _, state = key_tile(i, carry, True)
        o_t = jnp.concatenate([acc / l for (_, l, acc) in state], axis=0)
        o_ref[0] = o_t.T.astype(o_ref.dtype)

    @pl.when(has_ties)
    def _():
        attend(True)

    @pl.when(jnp.logical_not(has_ties))
    def _():
        attend(False)


def _memkv_kernel(mem_ref, g_ref, wk_ref, wv_ref, gk_ref, mk_ref, mv_ref):
    m = mem_ref[0]
    mn = (m * lax.rsqrt(jnp.mean(m * m, axis=-1, keepdims=True) + EPS) * g_ref[...]).astype(BF16)
    k = jnp.dot(mn, wk_ref[...], preferred_element_type=F32)
    v = jnp.dot(mn, wv_ref[...], preferred_element_type=F32)
    gk = gk_ref[...]
    for hd in range(MEM_HEADS):
        sl = slice(hd * MEM_HEAD_DIM, (hd + 1) * MEM_HEAD_DIM)
        kh = k[:, sl]
        kh = kh * lax.rsqrt(jnp.mean(kh * kh, axis=-1, keepdims=True) + EPS) * gk
        mk_ref[0, hd] = kh.astype(BF16)
        mv_ref[0, hd] = v[:, sl].astype(BF16)


def _post_kernel(x_ref, om_ref, od_ref, wout_ref, gx_ref, wq_ref, gq_ref, mk_ref, mv_ref, wo_ref,
                 o_ref):
    mixed = jnp.concatenate([om_ref[0], od_ref[0]], axis=-1)
    x1 = x_ref[0] + jnp.dot(mixed, wout_ref[...], preferred_element_type=F32)
    h = (x1 * lax.rsqrt(jnp.mean(x1 * x1, axis=-1, keepdims=True) + EPS) * gx_ref[...]).astype(BF16)
    q = jnp.dot(h, wq_ref[...], preferred_element_type=F32)
    gq = gq_ref[...]
    outs = []
    for hd in range(MEM_HEADS):
        qh = q[:, hd * MEM_HEAD_DIM:(hd + 1) * MEM_HEAD_DIM]
        qh = qh * lax.rsqrt(jnp.mean(qh * qh, axis=-1, keepdims=True) + EPS) * gq
        s = _nt_dot((qh * (MEM_HEAD_DIM ** -0.5)).astype(BF16), mk_ref[0, hd])
        p = jnp.exp(s - jnp.max(s, axis=-1, keepdims=True))
        p = p / jnp.sum(p, axis=-1, keepdims=True)
        outs.append(jnp.dot(p.astype(BF16), mv_ref[0, hd], preferred_element_type=F32))
    o = jnp.concatenate(outs, axis=-1).astype(BF16)
    o_ref[0] = x1 + jnp.dot(o, wo_ref[...], preferred_element_type=F32)


def _ffn_kernel(x_ref, g_ref, wg_ref, wu_ref, wd_ref, o_ref, *, chunk):
    x = x_ref[...]
    h = (x * lax.rsqrt(jnp.mean(x * x, axis=-1, keepdims=True) + EPS) * g_ref[...]).astype(BF16)
    acc = x
    for c in range(D_FF // chunk):
        sl = slice(c * chunk, (c + 1) * chunk)
        gate = jnp.dot(h, wg_ref[:, sl], preferred_element_type=F32)
        up = jnp.dot(h, wu_ref[:, sl], preferred_element_type=F32)
        act = gate * (1.0 / (1.0 + jnp.exp(-gate))) * up
        acc = acc + jnp.dot(act.astype(BF16), wd_ref[sl, :], preferred_element_type=F32)
    o_ref[...] = acc


def _full(shape):
    return pl.BlockSpec(shape, lambda *_: (0,) * len(shape))


def _params(n_axes):
    return pltpu.CompilerParams(dimension_semantics=("arbitrary",) * n_axes,
                                vmem_limit_bytes=VMEM_LIMIT_BYTES)


def _rope_tables(positions):
    pos = positions.astype(F32)[..., None]

    def cs(rot_dim):
        inv_freq = ROPE_THETA ** (-jnp.arange(0, rot_dim, 2, dtype=F32) / rot_dim)
        ang = pos * inv_freq
        return jnp.cos(ang), jnp.sin(ang)

    b, s = positions.shape
    one = lambda n: jnp.ones((b, s, n), F32)
    zero = lambda n: jnp.zeros((b, s, n), F32)
    c, sn = cs(MLA_ROPE_DIM)
    cm = jnp.concatenate([one(64), c, c, one(32)], -1)
    am = jnp.concatenate([zero(64), -sn, zero(48)], -1)
    bm = jnp.concatenate([zero(80), sn, zero(32)], -1)
    c, sn = cs(PARTIAL_ROT_DIM)
    cd = jnp.concatenate([c, c, one(48)] * 2, -1)
    ad = jnp.concatenate([-sn, zero(56)] * 2, -1)
    bd = jnp.concatenate([zero(8), sn, zero(48)] * 2, -1)
    return cm, am, bm, cd, ad, bd


def _pack_weights(w_in, mla_w_uq, mla_w_ukv):
    d = w_in.shape[0]
    z = lambda n: jnp.zeros((d, n), w_in.dtype)
    col = lambda name: w_in[:, _OFF[name][0]:_OFF[name][1]]
    win = jnp.concatenate([col("c_q"), col("c_kv"), z(64), col("k_pe"), z(32), col("q_d"), col("k_d"),
                           col("q_i"), col("k_i"), z(64)], axis=1)
    wvd_t = col("v_d").T
    wwi_t = jnp.pad(col("w_i").T, ((0, PACKED_ROWS - IDX_HEADS), (0, 0)))
    uq = mla_w_uq.reshape(MLA_Q_RANK, MLA_HEADS, MLA_QK_DIM)
    wuq = jnp.pad(uq, ((0, 0), (0, 0), (0, LANES - MLA_QK_DIM))).reshape(MLA_Q_RANK, MLA_HEADS * LANES)
    ukv = mla_w_ukv.reshape(MLA_KV_RANK, MLA_HEADS, MLA_NOPE_DIM + MLA_V_DIM)
    wukvk = jnp.pad(ukv[:, :, :MLA_NOPE_DIM], ((0, 0), (0, 0), (0, LANES - MLA_NOPE_DIM)))
    wukvk = wukvk.reshape(MLA_KV_RANK, MLA_HEADS * LANES)
    wukvv = ukv[:, :, MLA_NOPE_DIM:].reshape(MLA_KV_RANK, MLA_HEADS * MLA_V_DIM).T
    return [w.astype(BF16) for w in (win, wvd_t, wwi_t, wuq, wukvk, wukvv)]


def _pad_gain(g, n):
    return jnp.pad(g, (0, n - g.shape[0])).reshape(1, n)


def kernel(x, mem, positions, norm_mix, w_in, mla_q_a_norm, mla_w_uq, mla_kv_a_norm, mla_w_ukv, mla_q_norm, mla_k_norm, dsa_q_norm, dsa_k_norm, idx_k_norm, w_out, norm_mem_x, norm_mem_kv, mem_w_q, mem_w_k, mem_w_v, mem_q_norm, mem_k_norm, mem_w_o, norm_ffn, ffn_w_gate, ffn_w_up, ffn_w_down):
    B, S, D = x.shape
    n_mem = mem.shape[1]
    top_k = min(INDEX_TOPK_MAX, S // 4)
    depth = norm_mix.shape[0]
    tables = _rope_tables(positions)

    ts = min(512, S)
    t_mla = ts
    t_dsa = min(256, S)
    mla_hps = 4
    mem_w = MEM_HEADS * MEM_HEAD_DIM

    for l in range(depth):
        win, wvd_t, wwi_t, wuq, wukvk, wukvv = _pack_weights(w_in[l], mla_w_uq[l], mla_w_ukv[l])
        row = lambda a: a.reshape(1, -1)

        tok = lambda w: pl.BlockSpec((1, ts, w), lambda b, i: (b, i, 0))
        heads = lambda h, w: pl.BlockSpec((1, h, ts, w), lambda b, i: (b, 0, i, 0))
        sds = jax.ShapeDtypeStruct
        proj_out = pl.pallas_call(
            functools.partial(_proj_kernel, dsa_key_block=t_dsa),
            grid=(B, S // ts),
            in_specs=[tok(D), _full((1, D)), _full((D, P_COLS)),
                      _full((DSA_KV_HEADS * DSA_HEAD_DIM, D)), _full((PACKED_ROWS, D)), _full((1, MLA_Q_RANK)),
                      _full((MLA_Q_RANK, MLA_HEADS * LANES)), _full((1, MLA_KV_RANK)),
                      _full((MLA_KV_RANK, MLA_HEADS * LANES)), _full((MLA_HEADS * MLA_V_DIM, MLA_KV_RANK)),
                      _full((1, LANES)), _full((1, LANES)), _full((1, LANES)), _full((1, LANES)),
                      _full((1, LANES))] + [tok(LANES)] * 6,
            out_specs=[heads(MLA_HEADS, LANES), heads(MLA_HEADS, LANES),
                       pl.BlockSpec((1, MLA_HEADS, 1, MLA_V_DIM, ts), lambda b, i: (b, 0, i, 0, 0)),
                       heads(DSA_HEADS, DSA_HEAD_DIM), heads(DSA_KV_HEADS, DSA_HEAD_DIM),
                       pl.BlockSpec((1, DSA_KV_HEADS, ts // t_dsa, DSA_HEAD_DIM, t_dsa),
                                    lambda b, i: (b, 0, i, 0, 0)),
                       heads(IDX_HEADS, IDX_DIM), tok(IDX_DIM),
                       pl.BlockSpec((1, IDX_HEADS, ts), lambda b, i: (b, 0, i))],
            out_shape=[sds((B, MLA_HEADS, S, LANES), BF16), sds((B, MLA_HEADS, S, LANES), BF16),
                       sds((B, MLA_HEADS, S // ts, MLA_V_DIM, ts), BF16),
                       sds((B, DSA_HEADS, S, DSA_HEAD_DIM), BF16),
                       sds((B, DSA_KV_HEADS, S, DSA_HEAD_DIM), BF16),
                       sds((B, DSA_KV_HEADS, S // t_dsa, DSA_HEAD_DIM, t_dsa), BF16),
                       sds((B, IDX_HEADS, S, IDX_DIM), BF16),
                       sds((B, S, IDX_DIM), BF16), sds((B, IDX_HEADS, S), F32)],
            compiler_params=_params(2),
            name="proj",
        )(x, row(norm_mix[l]), win, wvd_t, wwi_t, row(mla_q_a_norm[l]), wuq, row(mla_kv_a_norm[l]), wukvk, wukvv,
          _pad_gain(mla_q_norm[l], LANES), _pad_gain(mla_k_norm[l], LANES),
          row(jnp.tile(dsa_q_norm[l], 2)), row(jnp.tile(dsa_k_norm[l], 2)),
          _pad_gain(idx_k_norm[l], LANES), *tables)
        qm, km, vm, qd, kd, vd, qi, ki, wi = proj_out

        o_mla = pl.pallas_call(
            functools.partial(_mla_kernel, tile=t_mla, heads=mla_hps),
            grid=(B, MLA_HEADS // mla_hps, S // t_mla),
            in_specs=[pl.BlockSpec((1, mla_hps, t_mla, LANES), lambda b, h, i: (b, h, i, 0)),
                      pl.BlockSpec((1, mla_hps, S, LANES), lambda b, h, i: (b, h, 0, 0)),
                      pl.BlockSpec((1, mla_hps, S // t_mla, MLA_V_DIM, t_mla),
                                   lambda b, h, i: (b, h, 0, 0, 0))],
            out_specs=pl.BlockSpec((1, t_mla, mla_hps * MLA_V_DIM), lambda b, h, i: (b, i, h)),
            out_shape=sds((B, S, MLA_HEADS * MLA_V_DIM), BF16),
            scratch_shapes=[pltpu.VMEM((t_mla, t_mla), F32)],
            compiler_params=_params(3),
            name="mla_attn",
        )(qm, km, vm)

        o_dsa = pl.pallas_call(
            functools.partial(_dsa_kernel, tile=t_dsa, top_k=top_k),
            grid=(B, S // t_dsa),
            in_specs=[pl.BlockSpec((1, IDX_HEADS, t_dsa, IDX_DIM), lambda b, i: (b, 0, i, 0)),
                      pl.BlockSpec((1, IDX_HEADS, t_dsa), lambda b, i: (b, 0, i)),
                      pl.BlockSpec((1, S, IDX_DIM), lambda b, i: (b, 0, 0)),
                      pl.BlockSpec((1, DSA_HEADS, t_dsa, DSA_HEAD_DIM), lambda b, i: (b, 0, i, 0)),
                      pl.BlockSpec((1, DSA_KV_HEADS, S, DSA_HEAD_DIM), lambda b, i: (b, 0, 0, 0)),
                      pl.BlockSpec((1, DSA_KV_HEADS, S // t_dsa, DSA_HEAD_DIM, t_dsa),
                                   lambda b, i: (b, 0, 0, 0, 0))],
            out_specs=pl.BlockSpec((1, t_dsa, DSA_HEADS * DSA_HEAD_DIM), lambda b, i: (b, i, 0)),
            out_shape=sds((B, S, DSA_HEADS * DSA_HEAD_DIM), BF16),
            scratch_shapes=[pltpu.VMEM((S // t_dsa, t_dsa, t_dsa), jnp.int32),
                            pltpu.VMEM((S // t_dsa, t_dsa, t_dsa), jnp.int16),
                            pltpu.VMEM((S // t_dsa, t_dsa, t_dsa), jnp.int16),
                            pltpu.VMEM((t_dsa, DSA_GROUP * t_dsa), F32)],
            compiler_params=_params(2),
            name="dsa",
        )(qi, wi, ki, qd, kd, vd)

        mk, mv = pl.pallas_call(
            _memkv_kernel,
            grid=(B,),
            in_specs=[pl.BlockSpec((1, n_mem, D), lambda b: (b, 0, 0)), _full((1, D)),
                      _full((D, mem_w)), _full((D, mem_w)), _full((1, MEM_HEAD_DIM))],
            out_specs=[pl.BlockSpec((1, MEM_HEADS, n_mem, MEM_HEAD_DIM), lambda b: (b, 0, 0, 0))] * 2,
            out_shape=[sds((B, MEM_HEADS, n_mem, MEM_HEAD_DIM), BF16)] * 2,
            compiler_params=_params(1),
            name="mem_kv",
        )(mem, row(norm_mem_kv[l]), mem_w_k[l].astype(BF16), mem_w_v[l].astype(BF16),
          row(mem_k_norm[l]))

        memspec = pl.BlockSpec((1, MEM_HEADS, n_mem, MEM_HEAD_DIM), lambda b, i: (b, 0, 0, 0))
        x = pl.pallas_call(
            _post_kernel,
            grid=(B, S // ts),
            in_specs=[tok(D), tok(MLA_HEADS * MLA_V_DIM), tok(DSA_HEADS * DSA_HEAD_DIM),
                      _full((D, D)), _full((1, D)), _full((D, mem_w)), _full((1, MEM_HEAD_DIM)),
                      memspec, memspec, _full((mem_w, D))],
            out_specs=tok(D),
            out_shape=sds((B, S, D), F32),
            compiler_params=_params(2),
            name="post",
        )(x, o_mla, o_dsa, w_out[l].astype(BF16), row(norm_mem_x[l]), mem_w_q[l].astype(BF16),
          row(mem_q_norm[l]), mk, mv, mem_w_o[l].astype(BF16))

        x = pl.pallas_call(
            functools.partial(_ffn_kernel, chunk=D_FF // 2),
            grid=(B * S // ts,),
            in_specs=[pl.BlockSpec((ts, D), lambda i: (i, 0)), _full((1, D)), _full((D, D_FF)),
                      _full((D, D_FF)), _full((D_FF, D))],
            out_specs=pl.BlockSpec((ts, D), lambda i: (i, 0)),
            out_shape=sds((B * S, D), F32),
            compiler_params=_params(1),
            name="ffn",
        )(x.reshape(B * S, D), row(norm_ffn[l]), ffn_w_gate[l].astype(BF16), ffn_w_up[l].astype(BF16),
          ffn_w_down[l].astype(BF16)).reshape(B, S, D)
    return x
```

```python
import functools

import jax
import jax.numpy as jnp
from jax import lax
from jax.experimental import pallas as pl
from jax.experimental.pallas import tpu as pltpu

F32 = jnp.float32
BF16 = jnp.bfloat16

D_MODEL = 1024
MLA_HEADS = 8
MLA_Q_RANK = 256
MLA_KV_RANK = 128
MLA_NOPE_DIM = 64
MLA_ROPE_DIM = 32
MLA_V_DIM = 64
MLA_QK_DIM = MLA_NOPE_DIM + MLA_ROPE_DIM
DSA_HEADS = 8
DSA_KV_HEADS = 2
DSA_GROUP = DSA_HEADS // DSA_KV_HEADS
DSA_HEAD_DIM = 64
IDX_HEADS = 8
IDX_DIM = 64
PARTIAL_ROT_DIM = DSA_HEAD_DIM // 4
INDEX_TOPK_MAX = 256
MEM_HEADS = 4
MEM_HEAD_DIM = 128
D_FF = 2816
ROPE_THETA = 500000.0
EPS = 1e-6
LOG2_E = 1.4426950408889634

LANES = 128
VMEM_LIMIT_BYTES = 56 * 1024 * 1024

INT_MIN = -2 ** 31
INT16_MIN, INT16_MAX = -2 ** 15, 2 ** 15 - 1
NEG_INF_KEY = -2139095041
PACKED_ROWS = 16
SUM_ROWS = PACKED_ROWS
SAFE_LOG2_LOGIT = 80.0
BF16_ROUNDING_MARGIN = 1.02

_OFF = {}
_o = 0
for _name, _size in (("c_q", MLA_Q_RANK), ("c_kv", MLA_KV_RANK), ("k_pe", MLA_ROPE_DIM),
                     ("q_d", DSA_HEADS * DSA_HEAD_DIM), ("k_d", DSA_KV_HEADS * DSA_HEAD_DIM),
                     ("v_d", DSA_KV_HEADS * DSA_HEAD_DIM), ("q_i", IDX_HEADS * IDX_DIM),
                     ("k_i", IDX_DIM), ("w_i", IDX_HEADS)):
    _OFF[_name] = (_o, _o + _size)
    _o += _size

P_CQ = (0, 256)
P_CKV = (256, 384)
P_KPE = (384, 512)
P_QD = (512, 1024)
P_KD = (1024, 1152)
P_QI = (1152, 1664)
P_KI = (1664, 1792)
P_COLS = 1792


def _nt_dot(a, b):
    return lax.dot_general(a, b, (((1,), (1,)), ((), ())), preferred_element_type=F32)


def _rope(slab, c, a, b, shift):
    up = pltpu.roll(slab, LANES - shift, axis=1)
    dn = pltpu.roll(slab, shift, axis=1)
    return slab * c + up * a + dn * b


def _proj_kernel(x_ref, gmix_ref, win_ref, wvd_ref, wwi_ref, gqa_ref, wuq_ref, gkva_ref, wukvk_ref,
                 wukvv_ref, gq_ref, gk_ref, gdq_ref, gdk_ref, gik_ref,
                 cm_ref, am_ref, bm_ref, cd_ref, ad_ref, bd_ref,
                 qm_ref, km_ref, vm_ref, qd_ref, kd_ref, vd_ref, qi_ref, ki_ref, wi_ref, *, dsa_key_block):
    x = x_ref[0]
    h = (x * lax.rsqrt(jnp.mean(x * x, axis=-1, keepdims=True) + EPS) * gmix_ref[...]).astype(BF16)
    y = jnp.dot(h, win_ref[...], preferred_element_type=F32)

    cm, am, bm = cm_ref[0], am_ref[0], bm_ref[0]
    cd, ad, bd = cd_ref[0], ad_ref[0], bd_ref[0]
    lane = lax.broadcasted_iota(jnp.int32, (1, LANES), 1)
    lo_half = lane < DSA_HEAD_DIM

    c_q = y[:, P_CQ[0]:P_CQ[1]]
    cqn = c_q * lax.rsqrt(jnp.mean(c_q * c_q, axis=-1, keepdims=True) + EPS) * gqa_ref[...]
    qm_all = jnp.dot(cqn.astype(BF16), wuq_ref[...], preferred_element_type=F32)
    c_kv = y[:, P_CKV[0]:P_CKV[1]]
    ckvn = (c_kv * lax.rsqrt(jnp.mean(c_kv * c_kv, axis=-1, keepdims=True) + EPS)
            * gkva_ref[...]).astype(BF16)
    kn_all = jnp.dot(ckvn, wukvk_ref[...], preferred_element_type=F32)
    vm_t = _nt_dot(wukvv_ref[...], ckvn).astype(BF16)
    kpe = y[:, P_KPE[0]:P_KPE[1]]
    gq, gk = gq_ref[...], gk_ref[...]
    ones_rows = jnp.ones((SUM_ROWS, vm_t.shape[1]), BF16)
    for hd in range(MLA_HEADS):
        vm_ref[0, hd, 0, :MLA_V_DIM] = vm_t[hd * MLA_V_DIM:(hd + 1) * MLA_V_DIM]
        vm_ref[0, hd, 0, MLA_V_DIM:] = ones_rows
    for hd in range(MLA_HEADS):
        sl = slice(hd * LANES, (hd + 1) * LANES)
        q = qm_all[:, sl]
        q = q * lax.rsqrt(jnp.sum(q * q, axis=-1, keepdims=True) * (1.0 / MLA_QK_DIM) + EPS) * gq
        q = _rope(q, cm, am, bm, MLA_ROPE_DIM // 2) * (MLA_QK_DIM ** -0.5 * LOG2_E)
        qm_ref[0, hd] = q.astype(BF16)
        k = kn_all[:, sl] + kpe
        k = k * lax.rsqrt(jnp.sum(k * k, axis=-1, keepdims=True) * (1.0 / MLA_QK_DIM) + EPS) * gk
        k = _rope(k, cm, am, bm, MLA_ROPE_DIM // 2)
        km_ref[0, hd] = k.astype(BF16)

    def pair_norm(slab, g):
        sq = slab * slab
        s_lo = jnp.sum(jnp.where(lo_half, sq, 0.0), axis=-1, keepdims=True)
        s_hi = jnp.sum(jnp.where(lo_half, 0.0, sq), axis=-1, keepdims=True)
        r = jnp.where(lo_half, lax.rsqrt(s_lo * (1.0 / DSA_HEAD_DIM) + EPS),
                      lax.rsqrt(s_hi * (1.0 / DSA_HEAD_DIM) + EPS))
        return slab * r * g

    rot = PARTIAL_ROT_DIM // 2
    gdq = gdq_ref[...]
    for p in range(DSA_HEADS // 2):
        slab = y[:, P_QD[0] + p * LANES:P_QD[0] + (p + 1) * LANES]
        slab = _rope(pair_norm(slab, gdq), cd, ad, bd, rot) * (DSA_HEAD_DIM ** -0.5 * LOG2_E)
        qd_ref[0, 2 * p] = slab[:, :DSA_HEAD_DIM].astype(BF16)
        qd_ref[0, 2 * p + 1] = slab[:, DSA_HEAD_DIM:].astype(BF16)
        slab = y[:, P_QI[0] + p * LANES:P_QI[0] + (p + 1) * LANES]
        slab = _rope(slab, cd, ad, bd, rot) * (IDX_DIM ** -0.5)
        qi_ref[0, 2 * p] = slab[:, :IDX_DIM].astype(BF16)
        qi_ref[0, 2 * p + 1] = slab[:, IDX_DIM:].astype(BF16)
    slab = _rope(pair_norm(y[:, P_KD[0]:P_KD[1]], gdk_ref[...]), cd, ad, bd, rot)
    kd_ref[0, 0] = slab[:, :DSA_HEAD_DIM].astype(BF16)
    kd_ref[0, 1] = slab[:, DSA_HEAD_DIM:].astype(BF16)
    vd_t = _nt_dot(wvd_ref[...], h).astype(BF16)
    for g in range(DSA_KV_HEADS):
        for j in range(vd_t.shape[1] // dsa_key_block):
            vd_ref[0, g, j, :DSA_HEAD_DIM] = vd_t[g * DSA_HEAD_DIM:(g + 1) * DSA_HEAD_DIM,
                                                  j * dsa_key_block:(j + 1) * dsa_key_block]
            vd_ref[0, g, j, DSA_HEAD_DIM:] = ones_rows[:, :dsa_key_block]
    k_i = y[:, P_KI[0]:P_KI[1]]
    k_i = (k_i * lax.rsqrt(jnp.sum(k_i * k_i, axis=-1, keepdims=True) * (1.0 / IDX_DIM) + EPS)
           * gik_ref[...])
    k_i = _rope(k_i, cd, ad, bd, rot)
    ki_ref[0] = k_i[:, :IDX_DIM].astype(BF16)
    wi_ref[0] = _nt_dot(wwi_ref[...], h)[:IDX_HEADS] * (IDX_HEADS ** -0.5)


def _softmax_step_t(s, v_t, m, l, acc):
    m_new = jnp.maximum(m, jnp.max(s, axis=0, keepdims=True))
    alpha = jnp.exp2(m - m_new)
    p = jnp.exp2(s - m_new)
    l = alpha * l + jnp.sum(p, axis=0, keepdims=True)
    acc = alpha * acc + jnp.dot(v_t, p.astype(BF16), preferred_element_type=F32)
    return m_new, l, acc


def _exp_step_t(s, v_aug_t, acc):
    return (acc + jnp.dot(v_aug_t, jnp.exp2(s).astype(BF16), preferred_element_type=F32),)


def _online_step_t(v_dim, s, v_aug_t, m, l, acc):
    return _softmax_step_t(s, v_aug_t[:v_dim], m, l, acc)


def _attention_state(small_logits, v_dim, n):
    if small_logits:
        init = (jnp.zeros((v_dim + SUM_ROWS, n), F32),)
        return init, _exp_step_t, lambda acc: acc[:v_dim] / acc[v_dim:v_dim + 1]
    init = (jnp.full((1, n), -1e30, F32), jnp.zeros((1, n), F32), jnp.zeros((v_dim, n), F32))
    return init, functools.partial(_online_step_t, v_dim), lambda m, l, acc: acc / l


def _mla_kernel(small_ref, q_ref, k_ref, vt_ref, o_ref, s_scr, *, tile, heads):
    i = pl.program_id(2)
    qs = [q_ref[0, hd] for hd in range(heads)]
    visible = (lax.broadcasted_iota(jnp.int32, (tile, tile), 0)
               <= lax.broadcasted_iota(jnp.int32, (tile, tile), 1))

    def logits(hd, kt):
        k0 = pl.multiple_of(kt * tile, tile)
        return _nt_dot(k_ref[0, hd, pl.ds(k0, tile), :], qs[hd])

    def attend(small_logits):
        init, step, finish = _attention_state(small_logits, MLA_V_DIM, tile)

        def key_tile(kt, carry, diag):
            out = []
            s_next = None
            for hd in range(heads):
                s = s_scr[...] if hd == 0 else s_next
                if hd + 1 < heads:
                    s_next = logits(hd + 1, kt)
                elif not diag:
                    s_scr[...] = logits(0, kt + 1)
                if diag:
                    s = jnp.where(visible, s, -jnp.inf)
                out.append(step(s, vt_ref[0, hd, kt], *carry[hd]))
            return tuple(out)

        s_scr[...] = logits(0, 0)
        carry = lax.fori_loop(0, i, lambda kt, c: key_tile(kt, c, False), (init,) * heads)
        carry = key_tile(i, carry, True)
        o_t = jnp.concatenate([finish(*st) for st in carry], axis=0)
        o_ref[0] = o_t.T.astype(o_ref.dtype)

    @pl.when(small_ref[0] == 1)
    def _():
        attend(True)

    @pl.when(small_ref[0] != 1)
    def _():
        attend(False)


def _dsa_kernel(small_ref, qi_ref, wi_ref, ki_ref, qd_ref, kd_ref, vdt_ref, o_ref,
                key_scr, hi_scr, lo_scr, s_scr, *, tq, tk, top_k):
    i = pl.program_id(1)
    last = (i * tq) // tk
    n_tiles = last + 1
    visible = (last * tk + lax.broadcasted_iota(jnp.int32, (tk, tq), 0)
               <= i * tq + lax.broadcasted_iota(jnp.int32, (tk, tq), 1))

    w_t = wi_ref[0]
    pair = 2
    n_pairs = IDX_HEADS // pair

    def idx_logits(p, kt):
        k0 = pl.multiple_of(kt * tk, tk)
        q = qi_ref[0, p * pair:(p + 1) * pair].reshape(pair * tq, IDX_DIM)
        return _nt_dot(ki_ref[0, pl.ds(k0, tk), :], q)

    def score_tile(kt, diag):
        acc = jnp.zeros((tk, tq), F32)
        s_next = idx_logits(0, kt)
        for p in range(n_pairs):
            s = s_next
            if p + 1 < n_pairs:
                s_next = idx_logits(p + 1, kt)
            for j in range(pair):
                hd = p * pair + j
                acc = acc + jnp.maximum(s[:, j * tq:(j + 1) * tq], 0.0) * w_t[hd:hd + 1, :]
        if diag:
            acc = jnp.where(visible, acc, -jnp.inf)
        bits = pltpu.bitcast(acc, jnp.int32)
        bits = jnp.where(bits == INT_MIN, 0, bits)
        key = bits ^ ((bits >> 31) & 0x7FFFFFFF)
        key_scr[kt] = key
        hi_scr[kt] = (key >> 16).astype(jnp.int16)
        lo_scr[kt] = ((key & 0xFFFF) + INT16_MIN).astype(jnp.int16)

    def score_full(kt, c):
        score_tile(kt, False)
        return c

    lax.fori_loop(0, last, score_full, 0)
    score_tile(last, True)

    one16, zero16 = jnp.int16(1), jnp.int16(0)
    n_groups = tk // PACKED_ROWS
    n_acc = 4

    def row16(v):
        return jnp.broadcast_to(v, (PACKED_ROWS, tq)).astype(jnp.int16)

    def count_ge(scr, cand):
        c16 = row16(cand)

        def body(kt, cnts):
            cnts = list(cnts)
            for r in range(n_groups):
                blk = scr[kt, r * PACKED_ROWS:(r + 1) * PACKED_ROWS, :]
                cnts[r % n_acc] = cnts[r % n_acc] + jnp.where(blk >= c16, one16, zero16)
            return tuple(cnts)

        cnts = lax.fori_loop(0, n_tiles, body,
                             tuple(jnp.zeros((PACKED_ROWS, tq), jnp.int16) for _ in range(n_acc)))
        total = cnts[0].astype(jnp.int32)
        for c in cnts[1:]:
            total = total + c.astype(jnp.int32)
        return jnp.sum(total, axis=0, keepdims=True)

    def bisect(scr, need, n_all):
        n0 = count_ge(scr, jnp.zeros((1, tq), jnp.int32))
        ok = n0 >= need
        init = (jnp.where(ok, 0, INT16_MIN), jnp.where(ok, n0, n_all), jnp.where(ok, 0, n0))

        def step(j, carry):
            lo, n_ge, n_gt = carry
            cand = lo + jnp.left_shift(jnp.int32(1), 14 - j)
            cnt = count_ge(scr, cand)
            ok = cnt >= need
            return jnp.where(ok, cand, lo), jnp.where(ok, cnt, n_ge), jnp.where(ok, n_gt, cnt)

        return lax.fori_loop(0, 15, step, init)

    n_stored = jnp.full((1, tq), n_tiles * tk, jnp.int32)
    hi_thr, n_ge_hi, n_gt_hi = bisect(hi_scr, top_k, n_stored)
    hi16 = row16(hi_thr)

    def keep_bucket(kt, c):
        for r in range(n_groups):
            rows = slice(r * PACKED_ROWS, (r + 1) * PACKED_ROWS)
            lo_scr[kt, rows, :] = jnp.where(hi_scr[kt, rows, :] == hi16, lo_scr[kt, rows, :],
                                            jnp.int16(INT16_MIN))
        return c

    lax.fori_loop(0, n_tiles, keep_bucket, 0)
    lo_thr, n_ge_lo, n_gt_lo = bisect(lo_scr, top_k - n_gt_hi, n_ge_hi - n_gt_hi)
    thr = hi_thr * 65536 + (lo_thr - INT16_MIN)
    n_gt = n_gt_hi + n_gt_lo
    quota = jnp.where(thr > NEG_INF_KEY, top_k - n_gt, 0).astype(F32)
    has_ties = jnp.max(n_gt_hi + n_ge_lo) > top_k
    lo_sel = jnp.maximum(thr, NEG_INF_KEY + 1)

    def logits(g, kt):
        k0 = pl.multiple_of(kt * tk, tk)
        q = qd_ref[0, g * DSA_GROUP:(g + 1) * DSA_GROUP].reshape(DSA_GROUP * tq, DSA_HEAD_DIM)
        return _nt_dot(kd_ref[0, g, pl.ds(k0, tk), :], q)

    def attend(small_logits):
        with_ties = not small_logits
        init, step, finish = _attention_state(small_logits, DSA_HEAD_DIM, tq)

        def key_tile(kt, carry, final):
            eq_seen, state = carry
            keys = key_scr[kt]
            if with_ties:
                eq = keys == thr
                tri = (lax.broadcasted_iota(jnp.int32, (tk, tk), 1)
                       <= lax.broadcasted_iota(jnp.int32, (tk, tk), 0))
                rank = eq_seen + jnp.dot(jnp.where(tri, 1.0, 0.0).astype(BF16),
                                         jnp.where(eq, 1.0, 0.0).astype(BF16),
                                         preferred_element_type=F32)
                sel = jnp.where(eq, rank, jnp.where(keys > thr, 0.0, 3e38)) <= quota
                eq_seen = rank[tk - 1:tk, :]
            else:
                sel = keys >= lo_sel
            s_g = [s_scr[...], logits(1, kt)]
            new_state = []
            for g in range(DSA_KV_HEADS):
                if g == 1 and not final:
                    s_scr[...] = logits(0, kt + 1)
                for r in range(DSA_GROUP):
                    s = jnp.where(sel, s_g[g][:, r * tq:(r + 1) * tq], -jnp.inf)
                    new_state.append(step(s, vdt_ref[0, g, kt], *state[g * DSA_GROUP + r]))
            return eq_seen, tuple(new_state)

        s_scr[...] = logits(0, 0)
        carry = lax.fori_loop(0, last, lambda kt, c: key_tile(kt, c, False),
                              (jnp.zeros((1, tq), F32), (init,) * DSA_HEADS))
        _, state = key_tile(last, carry, True)
        o_t = jnp.concatenate([finish(*st) for st in state], axis=0)
        o_ref[0] = o_t.T.astype(o_ref.dtype)

    fast = jnp.logical_and(small_ref[0] == 1, jnp.logical_not(has_ties))

    @pl.when(fast)
    def _():
        attend(True)

    @pl.when(jnp.logical_not(fast))
    def _():
        attend(False)


def _memkv_kernel(mem_ref, g_ref, wk_ref, wv_ref, gk_ref, mk_ref, mv_ref):
    m = mem_ref[0]
    mn = (m * lax.rsqrt(jnp.mean(m * m, axis=-1, keepdims=True) + EPS) * g_ref[...]).astype(BF16)
    k = jnp.dot(mn, wk_ref[...], preferred_element_type=F32)
    v = jnp.dot(mn, wv_ref[...], preferred_element_type=F32)
    gk = gk_ref[...]
    for hd in range(MEM_HEADS):
        sl = slice(hd * MEM_HEAD_DIM, (hd + 1) * MEM_HEAD_DIM)
        kh = k[:, sl]
        kh = kh * lax.rsqrt(jnp.mean(kh * kh, axis=-1, keepdims=True) + EPS) * gk
        mk_ref[0, hd] = kh.astype(BF16)
        mv_ref[0, hd] = v[:, sl].astype(BF16)


def _post_kernel(x_ref, om_ref, od_ref, wout_ref, gx_ref, wq_ref, gq_ref, mk_ref, mv_ref, wo_ref,
                 o_ref):
    mixed = jnp.concatenate([om_ref[0], od_ref[0]], axis=-1)
    x1 = x_ref[0] + jnp.dot(mixed, wout_ref[...], preferred_element_type=F32)
    h = (x1 * lax.rsqrt(jnp.mean(x1 * x1, axis=-1, keepdims=True) + EPS) * gx_ref[...]).astype(BF16)
    q = jnp.dot(h, wq_ref[...], preferred_element_type=F32)
    gq = gq_ref[...]
    outs = []
    for hd in range(MEM_HEADS):
        qh = q[:, hd * MEM_HEAD_DIM:(hd + 1) * MEM_HEAD_DIM]
        qh = qh * lax.rsqrt(jnp.mean(qh * qh, axis=-1, keepdims=True) + EPS) * gq
        s = _nt_dot((qh * (MEM_HEAD_DIM ** -0.5)).astype(BF16), mk_ref[0, hd])
        p = jnp.exp(s - jnp.max(s, axis=-1, keepdims=True))
        p = p / jnp.sum(p, axis=-1, keepdims=True)
        outs.append(jnp.dot(p.astype(BF16), mv_ref[0, hd], preferred_element_type=F32))
    o = jnp.concatenate(outs, axis=-1).astype(BF16)
    o_ref[0] = x1 + jnp.dot(o, wo_ref[...], preferred_element_type=F32)


def _ffn_kernel(x_ref, g_ref, wg_ref, wu_ref, wd_ref, o_ref, *, chunk):
    x = x_ref[...]
    h = (x * lax.rsqrt(jnp.mean(x * x, axis=-1, keepdims=True) + EPS) * g_ref[...]).astype(BF16)
    acc = x
    for c in range(D_FF // chunk):
        sl = slice(c * chunk, (c + 1) * chunk)
        gate = jnp.dot(h, wg_ref[:, sl], preferred_element_type=F32)
        up = jnp.dot(h, wu_ref[:, sl], preferred_element_type=F32)
        act = gate * (1.0 / (1.0 + jnp.exp(-gate))) * up
        acc = acc + jnp.dot(act.astype(BF16), wd_ref[sl, :], preferred_element_type=F32)
    o_ref[...] = acc


def _full(shape):
    return pl.BlockSpec(shape, lambda *_: (0,) * len(shape))


def _params(n_axes):
    return pltpu.CompilerParams(dimension_semantics=("arbitrary",) * n_axes,
                                vmem_limit_bytes=VMEM_LIMIT_BYTES)


def _rope_tables(positions):
    pos = positions.astype(F32)[..., None]

    def cs(rot_dim):
        inv_freq = ROPE_THETA ** (-jnp.arange(0, rot_dim, 2, dtype=F32) / rot_dim)
        ang = pos * inv_freq
        return jnp.cos(ang), jnp.sin(ang)

    b, s = positions.shape
    one = lambda n: jnp.ones((b, s, n), F32)
    zero = lambda n: jnp.zeros((b, s, n), F32)
    c, sn = cs(MLA_ROPE_DIM)
    cm = jnp.concatenate([one(64), c, c, one(32)], -1)
    am = jnp.concatenate([zero(64), -sn, zero(48)], -1)
    bm = jnp.concatenate([zero(80), sn, zero(32)], -1)
    c, sn = cs(PARTIAL_ROT_DIM)
    cd = jnp.concatenate([c, c, one(48)] * 2, -1)
    ad = jnp.concatenate([-sn, zero(56)] * 2, -1)
    bd = jnp.concatenate([zero(8), sn, zero(48)] * 2, -1)
    return cm, am, bm, cd, ad, bd


def _pack_weights(w_in, mla_w_uq, mla_w_ukv):
    d = w_in.shape[0]
    z = lambda n: jnp.zeros((d, n), w_in.dtype)
    col = lambda name: w_in[:, _OFF[name][0]:_OFF[name][1]]
    win = jnp.concatenate([col("c_q"), col("c_kv"), z(64), col("k_pe"), z(32), col("q_d"), col("k_d"),
                           col("q_i"), col("k_i"), z(64)], axis=1)
    wvd_t = col("v_d").T
    wwi_t = jnp.pad(col("w_i").T, ((0, PACKED_ROWS - IDX_HEADS), (0, 0)))
    uq = mla_w_uq.reshape(MLA_Q_RANK, MLA_HEADS, MLA_QK_DIM)
    wuq = jnp.pad(uq, ((0, 0), (0, 0), (0, LANES - MLA_QK_DIM))).reshape(MLA_Q_RANK, MLA_HEADS * LANES)
    ukv = mla_w_ukv.reshape(MLA_KV_RANK, MLA_HEADS, MLA_NOPE_DIM + MLA_V_DIM)
    wukvk = jnp.pad(ukv[:, :, :MLA_NOPE_DIM], ((0, 0), (0, 0), (0, LANES - MLA_NOPE_DIM)))
    wukvk = wukvk.reshape(MLA_KV_RANK, MLA_HEADS * LANES)
    wukvv = ukv[:, :, MLA_NOPE_DIM:].reshape(MLA_KV_RANK, MLA_HEADS * MLA_V_DIM).T
    return [w.astype(BF16) for w in (win, wvd_t, wwi_t, wuq, wukvk, wukvv)]


def _small_logits(g_q, g_k, dim, scale):
    bound = dim * jnp.max(jnp.abs(g_q)) * jnp.max(jnp.abs(g_k)) * (scale * LOG2_E * BF16_ROUNDING_MARGIN)
    return (bound <= SAFE_LOG2_LOGIT).astype(jnp.int32).reshape(1)


def _pad_gain(g, n):
    return jnp.pad(g, (0, n - g.shape[0])).reshape(1, n)


def kernel(x, mem, positions, norm_mix, w_in, mla_q_a_norm, mla_w_uq, mla_kv_a_norm, mla_w_ukv, mla_q_norm, mla_k_norm, dsa_q_norm, dsa_k_norm, idx_k_norm, w_out, norm_mem_x, norm_mem_kv, mem_w_q, mem_w_k, mem_w_v, mem_q_norm, mem_k_norm, mem_w_o, norm_ffn, ffn_w_gate, ffn_w_up, ffn_w_down):
    B, S, D = x.shape
    n_mem = mem.shape[1]
    top_k = min(INDEX_TOPK_MAX, S // 4)
    depth = norm_mix.shape[0]
    tables = _rope_tables(positions)

    ts = min(512, S)
    t_mla = ts
    tq_dsa = min(256, S)
    tk_dsa = min(512, S)
    mla_hps = 4
    mem_w = MEM_HEADS * MEM_HEAD_DIM

    for l in range(depth):
        win, wvd_t, wwi_t, wuq, wukvk, wukvv = _pack_weights(w_in[l], mla_w_uq[l], mla_w_ukv[l])
        row = lambda a: a.reshape(1, -1)

        tok = lambda w: pl.BlockSpec((1, ts, w), lambda b, i: (b, i, 0))
        heads = lambda h, w: pl.BlockSpec((1, h, ts, w), lambda b, i: (b, 0, i, 0))
        sds = jax.ShapeDtypeStruct
        smem = pl.BlockSpec(memory_space=pltpu.SMEM)
        proj_out = pl.pallas_call(
            functools.partial(_proj_kernel, dsa_key_block=tk_dsa),
            grid=(B, S // ts),
            in_specs=[tok(D), _full((1, D)), _full((D, P_COLS)),
                      _full((DSA_KV_HEADS * DSA_HEAD_DIM, D)), _full((PACKED_ROWS, D)), _full((1, MLA_Q_RANK)),
                      _full((MLA_Q_RANK, MLA_HEADS * LANES)), _full((1, MLA_KV_RANK)),
                      _full((MLA_KV_RANK, MLA_HEADS * LANES)), _full((MLA_HEADS * MLA_V_DIM, MLA_KV_RANK)),
                      _full((1, LANES)), _full((1, LANES)), _full((1, LANES)), _full((1, LANES)),
                      _full((1, LANES))] + [tok(LANES)] * 6,
            out_specs=[heads(MLA_HEADS, LANES), heads(MLA_HEADS, LANES),
                       pl.BlockSpec((1, MLA_HEADS, 1, MLA_V_DIM + SUM_ROWS, ts), lambda b, i: (b, 0, i, 0, 0)),
                       heads(DSA_HEADS, DSA_HEAD_DIM), heads(DSA_KV_HEADS, DSA_HEAD_DIM),
                       pl.BlockSpec((1, DSA_KV_HEADS, ts // tk_dsa, DSA_HEAD_DIM + SUM_ROWS, tk_dsa),
                                    lambda b, i: (b, 0, i, 0, 0)),
                       heads(IDX_HEADS, IDX_DIM), tok(IDX_DIM),
                       pl.BlockSpec((1, IDX_HEADS, ts), lambda b, i: (b, 0, i))],
            out_shape=[sds((B, MLA_HEADS, S, LANES), BF16), sds((B, MLA_HEADS, S, LANES), BF16),
                       sds((B, MLA_HEADS, S // ts, MLA_V_DIM + SUM_ROWS, ts), BF16),
                       sds((B, DSA_HEADS, S, DSA_HEAD_DIM), BF16),
                       sds((B, DSA_KV_HEADS, S, DSA_HEAD_DIM), BF16),
                       sds((B, DSA_KV_HEADS, S // tk_dsa, DSA_HEAD_DIM + SUM_ROWS, tk_dsa), BF16),
                       sds((B, IDX_HEADS, S, IDX_DIM), BF16),
                       sds((B, S, IDX_DIM), BF16), sds((B, IDX_HEADS, S), F32)],
            compiler_params=_params(2),
            name="proj",
        )(x, row(norm_mix[l]), win, wvd_t, wwi_t, row(mla_q_a_norm[l]), wuq, row(mla_kv_a_norm[l]), wukvk, wukvv,
          _pad_gain(mla_q_norm[l], LANES), _pad_gain(mla_k_norm[l], LANES),
          row(jnp.tile(dsa_q_norm[l], 2)), row(jnp.tile(dsa_k_norm[l], 2)),
          _pad_gain(idx_k_norm[l], LANES), *tables)
        qm, km, vm, qd, kd, vd, qi, ki, wi = proj_out

        o_mla = pl.pallas_call(
            functools.partial(_mla_kernel, tile=t_mla, heads=mla_hps),
            grid=(B, MLA_HEADS // mla_hps, S // t_mla),
            in_specs=[smem, pl.BlockSpec((1, mla_hps, t_mla, LANES), lambda b, h, i: (b, h, i, 0)),
                      pl.BlockSpec((1, mla_hps, S, LANES), lambda b, h, i: (b, h, 0, 0)),
                      pl.BlockSpec((1, mla_hps, S // t_mla, MLA_V_DIM + SUM_ROWS, t_mla),
                                   lambda b, h, i: (b, h, 0, 0, 0))],
            out_specs=pl.BlockSpec((1, t_mla, mla_hps * MLA_V_DIM), lambda b, h, i: (b, i, h)),
            out_shape=sds((B, S, MLA_HEADS * MLA_V_DIM), BF16),
            scratch_shapes=[pltpu.VMEM((t_mla, t_mla), F32)],
            compiler_params=_params(3),
            name="mla_attn",
        )(_small_logits(mla_q_norm[l], mla_k_norm[l], MLA_QK_DIM, MLA_QK_DIM ** -0.5), qm, km, vm)

        o_dsa = pl.pallas_call(
            functools.partial(_dsa_kernel, tq=tq_dsa, tk=tk_dsa, top_k=top_k),
            grid=(B, S // tq_dsa),
            in_specs=[smem, pl.BlockSpec((1, IDX_HEADS, tq_dsa, IDX_DIM), lambda b, i: (b, 0, i, 0)),
                      pl.BlockSpec((1, IDX_HEADS, tq_dsa), lambda b, i: (b, 0, i)),
                      pl.BlockSpec((1, S, IDX_DIM), lambda b, i: (b, 0, 0)),
                      pl.BlockSpec((1, DSA_HEADS, tq_dsa, DSA_HEAD_DIM), lambda b, i: (b, 0, i, 0)),
                      pl.BlockSpec((1, DSA_KV_HEADS, S, DSA_HEAD_DIM), lambda b, i: (b, 0, 0, 0)),
                      pl.BlockSpec((1, DSA_KV_HEADS, S // tk_dsa, DSA_HEAD_DIM + SUM_ROWS, tk_dsa),
                                   lambda b, i: (b, 0, 0, 0, 0))],
            out_specs=pl.BlockSpec((1, tq_dsa, DSA_HEADS * DSA_HEAD_DIM), lambda b, i: (b, i, 0)),
            out_shape=sds((B, S, DSA_HEADS * DSA_HEAD_DIM), BF16),
            scratch_shapes=[pltpu.VMEM((S // tk_dsa, tk_dsa, tq_dsa), jnp.int32),
                            pltpu.VMEM((S // tk_dsa, tk_dsa, tq_dsa), jnp.int16),
                            pltpu.VMEM((S // tk_dsa, tk_dsa, tq_dsa), jnp.int16),
                            pltpu.VMEM((tk_dsa, DSA_GROUP * tq_dsa), F32)],
            compiler_params=_params(2),
            name="dsa",
        )(_small_logits(dsa_q_norm[l], dsa_k_norm[l], DSA_HEAD_DIM, DSA_HEAD_DIM ** -0.5), qi, wi, ki, qd, kd, vd)

        mk, mv = pl.pallas_call(
            _memkv_kernel,
            grid=(B,),
            in_specs=[pl.BlockSpec((1, n_mem, D), lambda b: (b, 0, 0)), _full((1, D)),
                      _full((D, mem_w)), _full((D, mem_w)), _full((1, MEM_HEAD_DIM))],
            out_specs=[pl.BlockSpec((1, MEM_HEADS, n_mem, MEM_HEAD_DIM), lambda b: (b, 0, 0, 0))] * 2,
            out_shape=[sds((B, MEM_HEADS, n_mem, MEM_HEAD_DIM), BF16)] * 2,
            compiler_params=_params(1),
            name="mem_kv",
        )(mem, row(norm_mem_kv[l]), mem_w_k[l].astype(BF16), mem_w_v[l].astype(BF16),
          row(mem_k_norm[l]))

        memspec = pl.BlockSpec((1, MEM_HEADS, n_mem, MEM_HEAD_DIM), lambda b, i: (b, 0, 0, 0))
        x = pl.pallas_call(
            _post_kernel,
            grid=(B, S // ts),
            in_specs=[tok(D), tok(MLA_HEADS * MLA_V_DIM), tok(DSA_HEADS * DSA_HEAD_DIM),
                      _full((D, D)), _full((1, D)), _full((D, mem_w)), _full((1, MEM_HEAD_DIM)),
                      memspec, memspec, _full((mem_w, D))],
            out_specs=tok(D),
            out_shape=sds((B, S, D), F32),
            compiler_params=_params(2),
            name="post",
        )(x, o_mla, o_dsa, w_out[l].astype(BF16), row(norm_mem_x[l]), mem_w_q[l].astype(BF16),
          row(mem_q_norm[l]), mk, mv, mem_w_o[l].astype(BF16))

        x = pl.pallas_call(
            functools.partial(_ffn_kernel, chunk=D_FF // 2),
            grid=(B * S // ts,),
            in_specs=[pl.BlockSpec((ts, D), lambda i: (i, 0)), _full((1, D)), _full((D, D_FF)),
                      _full((D, D_FF)), _full((D_FF, D))],
            out_specs=pl.BlockSpec((ts, D), lambda i: (i, 0)),
            out_shape=sds((B * S, D), F32),
            compiler_params=_params(1),
            name="ffn",
        )(x.reshape(B * S, D), row(norm_ffn[l]), ffn_w_gate[l].astype(BF16), ffn_w_up[l].astype(BF16),
          ffn_w_down[l].astype(BF16)).reshape(B, S, D)
    return x
```
